```python
import functools
import jax, jax.numpy as jnp
from jax import lax
import numpy as np

D_MODEL = 1024
BATCH = 8
SEQ = 2048
DEPTH = 4
DEC_BATCH = 128
DEC_SEQ = 1
PAST_LEN = 8192
PAGE_SIZE = 128

N_A_LAYERS = DEPTH // 2
N_B_LAYERS = DEPTH - N_A_LAYERS
CHUNK = 128
D_SGU = 2 * D_MODEL
SGU_GROUPS = 8
SGU_GROUP_DIM = D_SGU // SGU_GROUPS
N_HEADS = 16
QK_NOPE_DIM = 128
QK_ROPE_DIM = 64
V_HEAD_DIM = 128
Q_LORA_RANK = 384
KV_LORA_RANK = 256
ROPE_THETA = 10000.0
D_FF = -(-8 * D_MODEL // (3 * 256)) * 256
ATTN_SCALE = (QK_NOPE_DIM + QK_ROPE_DIM) ** -0.5
EPS = 1e-6
Q_BLOCK = 128

kernel_name = "yoco_chunk_sgu_mla_decoder_step"


def rmsnorm(x, g):
    xf = x.astype(jnp.float32)
    y = xf * lax.rsqrt(jnp.mean(xf * xf, axis=-1, keepdims=True) + EPS)
    return (y * g.astype(jnp.float32)).astype(x.dtype)


def rope(x, pos):
    inv = ROPE_THETA ** (-jnp.arange(0, QK_ROPE_DIM, 2, dtype=jnp.float32) / QK_ROPE_DIM)
    ang = pos[:, None] * inv[None, :]
    ang = ang.reshape((ang.shape[0],) + (1,) * (x.ndim - 3) + (ang.shape[1],))
    cos, sin = jnp.cos(ang), jnp.sin(ang)
    x1, x2 = jnp.split(x.astype(jnp.float32), 2, axis=-1)
    return jnp.concatenate([x1 * cos - x2 * sin, x2 * cos + x1 * sin], axis=-1).astype(x.dtype)


def swiglu(h, w_in, w_out):
    g, u = jnp.split(h @ w_in, 2, axis=-1)
    return (jax.nn.silu(g) * u) @ w_out


def chunk_sgu(h, w_in, v_norm, w_s, b_s, w_out):
    z = jax.nn.gelu(h @ w_in)
    u, v = jnp.split(z, 2, axis=-1)
    v = rmsnorm(v, v_norm)
    B, L, _ = v.shape
    Lp = -(-L // CHUNK) * CHUNK
    vp = jnp.pad(v, ((0, 0), (0, Lp - L), (0, 0)))
    vc = vp.reshape(B, Lp // CHUNK, CHUNK, SGU_GROUPS, SGU_GROUP_DIM)
    ws = w_s * jnp.tril(jnp.ones((CHUNK, CHUNK), dtype=w_s.dtype))
    mixed = jnp.einsum('gts,bcsgd->bctgd', ws, vc) + jnp.transpose(b_s)[:, :, None]
    mixed = mixed.reshape(B, Lp, D_SGU)[:, :L]
    return (u * mixed) @ w_out, v


def shared_kv(h, norm_g, w_dkv, lat_norm, pos):
    ckr = rmsnorm(h, norm_g) @ w_dkv
    c = rmsnorm(ckr[..., :KV_LORA_RANK], lat_norm)
    kr = rope(ckr[..., KV_LORA_RANK:], pos)
    return c, kr


def mla_query(hn, w_dq, q_norm, w_uq, w_uk, pos):
    B, L, _ = hn.shape
    cq = rmsnorm(hn @ w_dq, q_norm)
    q = (cq @ w_uq).reshape(B, L, N_HEADS, QK_NOPE_DIM + QK_ROPE_DIM)
    q_nope, q_rope = q[..., :QK_NOPE_DIM], q[..., QK_NOPE_DIM:]
    q_rope = rope(q_rope, pos)
    q_lat = jnp.einsum('blhd,chd->blhc', q_nope, w_uk)
    return q_lat, q_rope


def attend_prompt(q_lat, q_rope, c, kr):
    B, L = q_lat.shape[:2]
    qb = min(Q_BLOCK, L)
    nb = L // qb
    ql = q_lat.reshape(B, nb, qb, N_HEADS, KV_LORA_RANK).transpose(1, 0, 2, 3, 4)
    qr = q_rope.reshape(B, nb, qb, N_HEADS, QK_ROPE_DIM).transpose(1, 0, 2, 3, 4)
    kpos = jnp.arange(L)

    def block(args):
        i, qlb, qrb = args
        s = (jnp.einsum('bqhc,bkc->bhqk', qlb, c, preferred_element_type=jnp.float32)
             + jnp.einsum('bqhr,bkr->bhqk', qrb, kr, preferred_element_type=jnp.float32)) * ATTN_SCALE
        qpos = i * qb + jnp.arange(qb)
        s = jnp.where(kpos[None, :] <= qpos[:, None], s, -jnp.inf)
        p = jax.nn.softmax(s, axis=-1).astype(c.dtype)
        return jnp.einsum('bhqk,bkc->bqhc', p, c)

    o = lax.map(block, (jnp.arange(nb), ql, qr))
    return o.transpose(1, 0, 2, 3, 4).reshape(B, L, N_HEADS, KV_LORA_RANK)


def attend_sample(q_lat, q_rope, c_new, kr_new, c_past, kr_past):
    s_past = (jnp.einsum('bqhc,bkc->bhqk', q_lat, c_past, preferred_element_type=jnp.float32)
              + jnp.einsum('bqhr,bkr->bhqk', q_rope, kr_past, preferred_element_type=jnp.float32))
    s_new = (jnp.einsum('bqhc,bkc->bhqk', q_lat, c_new, preferred_element_type=jnp.float32)
             + jnp.einsum('bqhr,bkr->bhqk', q_rope, kr_new, preferred_element_type=jnp.float32))
    n = c_new.shape[1]
    s_new = jnp.where(jnp.tril(jnp.ones((n, n), dtype=bool)), s_new, -jnp.inf)
    s = jnp.concatenate([s_past, s_new], axis=-1) * ATTN_SCALE
    p = jax.nn.softmax(s, axis=-1).astype(c_new.dtype)
    P = c_past.shape[1]
    return (jnp.einsum('bhqk,bkc->bqhc', p[..., :P], c_past)
            + jnp.einsum('bhqk,bkc->bqhc', p[..., P:], c_new))


def trunk(x, pos, attend, norm_mix, sgu_w_in, sgu_v_norm, sgu_w_s, sgu_b_s, sgu_w_out,
          norm_ffn, ffn_w_in, ffn_w_out, kv_norm, w_dkv, kv_latent_norm, w_uk, w_uv,
          w_dq, q_norm, w_uq, w_o, final_norm):
    h = x
    v_rows = []
    c = kr = None
    for l in range(DEPTH):
        hn = rmsnorm(h, norm_mix[l])
        if l < N_A_LAYERS:
            a_out, v = chunk_sgu(hn, sgu_w_in[l], sgu_v_norm[l], sgu_w_s[l], sgu_b_s[l], sgu_w_out[l])
            h = h + a_out
            v_rows.append(v)
        else:
            j = l - N_A_LAYERS
            q_lat, q_rope = mla_query(hn, w_dq[j], q_norm[j], w_uq[j], w_uk, pos)
            o_lat = attend(q_lat, q_rope, c, kr)
            o = jnp.einsum('blhc,chv->blhv', o_lat, w_uv)
            h = h + o.reshape(o.shape[0], o.shape[1], N_HEADS * V_HEAD_DIM) @ w_o[j]
        h = h + swiglu(rmsnorm(h, norm_ffn[l]), ffn_w_in[l], ffn_w_out[l])
        if l == N_A_LAYERS - 1:
            c, kr = shared_kv(h, kv_norm, w_dkv, kv_latent_norm, pos)
    return rmsnorm(h, final_norm), c, kr, jnp.stack(v_rows)


def setup_inputs(seed: int = 0) -> dict:
    key = jax.random.key(seed)
    ks = jax.random.split(key, 26)
    f32 = jnp.float32

    def w(k, shape, fan_in):
        return jax.random.normal(k, shape, f32) * fan_in ** -0.5

    def gain(k, shape):
        return 1.0 + 0.05 * jax.random.normal(k, shape, f32)

    n_pages = PAST_LEN // PAGE_SIZE
    n_used = DEC_BATCH * n_pages
    n_phys = n_used + max(1, n_used // 4)
    perm = jax.random.permutation(ks[3], n_phys)[:n_used]
    page_table = perm.reshape(DEC_BATCH, n_pages).astype(jnp.int32)

    return {
        "x_prompt": jax.random.normal(ks[0], (BATCH, SEQ, D_MODEL), f32),
        "x_sample": jax.random.normal(ks[1], (DEC_BATCH, DEC_SEQ, D_MODEL), f32),
        "cache_kv_latent": jax.random.normal(ks[2], (n_phys, PAGE_SIZE, KV_LORA_RANK), f32),
        "cache_k_rope": jax.random.normal(ks[4], (n_phys, PAGE_SIZE, QK_ROPE_DIM), f32),
        "page_table": page_table,
        "norm_mix": gain(ks[5], (DEPTH, D_MODEL)),
        "sgu_w_in": w(ks[6], (N_A_LAYERS, D_MODEL, 2 * D_SGU), D_MODEL),
        "sgu_v_norm": gain(ks[7], (N_A_LAYERS, D_SGU)),
        "sgu_w_s": 0.5 * w(ks[8], (N_A_LAYERS, SGU_GROUPS, CHUNK, CHUNK), CHUNK),
        "sgu_b_s": 1.0 + 0.1 * jax.random.normal(ks[9], (N_A_LAYERS, SGU_GROUPS, CHUNK), f32),
        "sgu_w_out": w(ks[10], (N_A_LAYERS, D_SGU, D_MODEL), D_SGU),
        "norm_ffn": gain(ks[11], (DEPTH, D_MODEL)),
        "ffn_w_in": w(ks[12], (DEPTH, D_MODEL, 2 * D_FF), D_MODEL),
        "ffn_w_out": w(ks[13], (DEPTH, D_FF, D_MODEL), D_FF),
        "kv_norm": gain(ks[14], (D_MODEL,)),
        "w_dkv": w(ks[15], (D_MODEL, KV_LORA_RANK + QK_ROPE_DIM), D_MODEL),
        "kv_latent_norm": gain(ks[16], (KV_LORA_RANK,)),
        "w_uk": w(ks[17], (KV_LORA_RANK, N_HEADS, QK_NOPE_DIM), KV_LORA_RANK),
        "w_uv": w(ks[18], (KV_LORA_RANK, N_HEADS, V_HEAD_DIM), KV_LORA_RANK),
        "w_dq": w(ks[19], (N_B_LAYERS, D_MODEL, Q_LORA_RANK), D_MODEL),
        "q_norm": gain(ks[20], (N_B_LAYERS, Q_LORA_RANK)),
        "w_uq": w(ks[21], (N_B_LAYERS, Q_LORA_RANK, N_HEADS * (QK_NOPE_DIM + QK_ROPE_DIM)), Q_LORA_RANK),
        "w_o": w(ks[22], (N_B_LAYERS, N_HEADS * V_HEAD_DIM, D_MODEL), N_HEADS * V_HEAD_DIM),
        "final_norm": gain(ks[23], (D_MODEL,)),
    }


def reference(x_prompt, x_sample, cache_kv_latent, cache_k_rope, page_table,
              norm_mix, sgu_w_in, sgu_v_norm, sgu_w_s, sgu_b_s, sgu_w_out,
              norm_ffn, ffn_w_in, ffn_w_out, kv_norm, w_dkv, kv_latent_norm, w_uk, w_uv,
              w_dq, q_norm, w_uq, w_o, final_norm):
    params = (norm_mix, sgu_w_in, sgu_v_norm, sgu_w_s, sgu_b_s, sgu_w_out,
              norm_ffn, ffn_w_in, ffn_w_out, kv_norm, w_dkv, kv_latent_norm, w_uk, w_uv,
              w_dq, q_norm, w_uq, w_o, final_norm)

    pos_p = jnp.arange(x_prompt.shape[1], dtype=jnp.float32)
    y_prompt, kv_latent_prompt, k_rope_prompt, _ = trunk(x_prompt, pos_p, attend_prompt, *params)

    n_seq, n_pages = page_table.shape
    c_past = cache_kv_latent[page_table].reshape(n_seq, n_pages * PAGE_SIZE, KV_LORA_RANK)
    kr_past = cache_k_rope[page_table].reshape(n_seq, n_pages * PAGE_SIZE, QK_ROPE_DIM)
    pos_s = jnp.arange(x_sample.shape[1], dtype=jnp.float32) + PAST_LEN
    attend_s = functools.partial(attend_sample, c_past=c_past, kr_past=kr_past)
    y_sample, kv_latent_sample, k_rope_sample, sgu_v_sample = trunk(x_sample, pos_s, attend_s, *params)

    return (y_prompt, y_sample, kv_latent_prompt, k_rope_prompt, kv_latent_sample, k_rope_sample, sgu_v_sample)
```

```python
import functools

import jax
import jax.numpy as jnp
from jax import lax
from jax.experimental import pallas as pl
from jax.experimental.pallas import tpu as pltpu

D_MODEL = 1024
CHUNK = 128
D_SGU = 2 * D_MODEL
SGU_GROUPS = 8
SGU_GROUP_DIM = D_SGU // SGU_GROUPS
N_HEADS = 16
QK_NOPE_DIM = 128
QK_ROPE_DIM = 64
V_HEAD_DIM = 128
Q_LORA_RANK = 384
KV_LORA_RANK = 256
ROPE_THETA = 10000.0
ATTN_SCALE = (QK_NOPE_DIM + QK_ROPE_DIM) ** -0.5
EPS = 1e-6

LANES = 128
TOKEN_TILE = 3 * CHUNK
HEAD_DIM_PAD = 2 * LANES
Q_TILE = 256
VMEM_LIMIT = 56 * 1024 * 1024

F32 = jnp.float32
BF16 = jnp.bfloat16


def _rms(x, g):
    return x * lax.rsqrt(jnp.mean(x * x, axis=-1, keepdims=True) + EPS) * g


def _const_spec(shape):
    zeros = (0,) * len(shape)
    return pl.BlockSpec(shape, lambda *_: zeros, pipeline_mode=pl.Buffered(1))


def _params(sem="arbitrary", n=1):
    return pltpu.CompilerParams(dimension_semantics=(sem,) * n, vmem_limit_bytes=VMEM_LIMIT)


def _sgu_kernel(h_ref, g_ref, win_ref, vn_ref, ws_ref, bt_ref, wout_ref,
                o_ref, vs_ref, wmix_ref, bmix_ref, *, sample_chunk):
    i = pl.program_id(0)

    @pl.when(i == 0)
    def _():
        r = lax.broadcasted_iota(jnp.int32, (CHUNK, CHUNK), 0)
        c = lax.broadcasted_iota(jnp.int32, (CHUNK, CHUNK), 1)
        for g in range(SGU_GROUPS):
            w = ws_ref[g]
            wmix_ref[0, g] = jnp.where(c <= r, w, 0.0).astype(BF16)
            wmix_ref[1, g] = jnp.where(c == r, w[0:1, 0:1], 0.0).astype(BF16)
        bmix_ref[0] = bt_ref[...]
        bmix_ref[1] = jnp.broadcast_to(bt_ref[0:1, :], bt_ref.shape)

    x = h_ref[...]
    hn = _rms(x, g_ref[...]).astype(BF16)
    z = jax.nn.gelu(jnp.dot(hn, win_ref[...], preferred_element_type=F32))
    u = z[:, :D_SGU]
    v = _rms(z[:, D_SGU:], vn_ref[...])
    vb = v.astype(BF16)
    ys = []
    for c in range(TOKEN_TILE // CHUNK):
        rows = slice(c * CHUNK, (c + 1) * CHUNK)
        is_sample = i * (TOKEN_TILE // CHUNK) + c == sample_chunk
        sel = is_sample.astype(jnp.int32)

        @pl.when(is_sample)
        def _():
            vs_ref[...] = v[rows]

        bias = bmix_ref[sel]
        parts = []
        for g in range(SGU_GROUPS):
            cols = slice(g * SGU_GROUP_DIM, (g + 1) * SGU_GROUP_DIM)
            m = jnp.dot(wmix_ref[sel, g], vb[rows, cols], preferred_element_type=F32)
            parts.append(m + bias[:, g:g + 1])
        ys.append((u[rows] * jnp.concatenate(parts, axis=1)).astype(BF16))
    y = jnp.concatenate(ys, axis=0)
    o_ref[...] = x + jnp.dot(y, wout_ref[...], preferred_element_type=F32)


def _sgu_layer(h, g, w_in, v_norm, w_s, b_t, w_out, n_sample):
    t = h.shape[0]
    tile = pl.BlockSpec((TOKEN_TILE, D_MODEL), lambda i: (i, 0))
    return pl.pallas_call(
        functools.partial(_sgu_kernel, sample_chunk=(t - n_sample) // CHUNK),
        grid=(t // TOKEN_TILE,),
        in_specs=[tile, _const_spec((1, D_MODEL)), _const_spec((D_MODEL, 2 * D_SGU)),
                  _const_spec((1, D_SGU)), _const_spec((SGU_GROUPS, CHUNK, CHUNK)),
                  _const_spec((CHUNK, SGU_GROUPS)), _const_spec((D_SGU, D_MODEL))],
        out_specs=[tile, pl.BlockSpec((n_sample, D_SGU), lambda i: (0, 0))],
        out_shape=[jax.ShapeDtypeStruct((t, D_MODEL), F32),
                   jax.ShapeDtypeStruct((n_sample, D_SGU), F32)],
        scratch_shapes=[pltpu.VMEM((2, SGU_GROUPS, CHUNK, CHUNK), BF16),
                        pltpu.VMEM((2, CHUNK, SGU_GROUPS), F32)],
        compiler_params=_params(),
        name="sgu_mixer",
    )(h, g, w_in, v_norm, w_s, b_t, w_out)


def _ffn_kernel(*refs, d_ff, n_split, has_attn, has_final):
    refs = list(refs)
    h_ref = refs.pop(0)
    x = h_ref[...]
    if has_attn:
        o_ref, wo_ref = refs.pop(0), refs.pop(0)
        x = x + jnp.dot(o_ref[...], wo_ref[...], preferred_element_type=F32)
    g_ref, win_ref, wout_ref = refs.pop(0), refs.pop(0), refs.pop(0)
    fin_ref = refs.pop(0) if has_final else None
    out_ref = refs.pop(0)
    hn = _rms(x, g_ref[...]).astype(BF16)
    acc = x
    width = d_ff // n_split
    for j in range(n_split):
        gate = jnp.dot(hn, win_ref[:, j * width:(j + 1) * width], preferred_element_type=F32)
        up = jnp.dot(hn, win_ref[:, d_ff + j * width:d_ff + (j + 1) * width],
                     preferred_element_type=F32)
        a = (jax.nn.silu(gate) * up).astype(BF16)
        acc = acc + jnp.dot(a, wout_ref[j * width:(j + 1) * width, :], preferred_element_type=F32)
    if has_final:
        acc = _rms(acc, fin_ref[...])
    out_ref[...] = acc


def _ffn_layer(h, g, w_in, w_out, attn=None, final=None):
    t = h.shape[0]
    d_ff = w_out.shape[0]
    tile = pl.BlockSpec((TOKEN_TILE, D_MODEL), lambda i: (i, 0))
    args, specs = [h], [tile]
    if attn is not None:
        o, w_o = attn
        args += [o, w_o]
        specs += [pl.BlockSpec((TOKEN_TILE, o.shape[1]), lambda i: (i, 0)), _const_spec(w_o.shape)]
    args += [g, w_in, w_out]
    specs += [_const_spec((1, D_MODEL)), _const_spec(w_in.shape), _const_spec(w_out.shape)]
    if final is not None:
        args.append(final)
        specs.append(_const_spec((1, D_MODEL)))
    return pl.pallas_call(
        functools.partial(_ffn_kernel, d_ff=d_ff, n_split=2, has_attn=attn is not None,
                          has_final=final is not None),
        grid=(t // TOKEN_TILE,),
        in_specs=specs,
        out_specs=tile,
        out_shape=jax.ShapeDtypeStruct((t, D_MODEL), F32),
        compiler_params=_params(),
        name="ffn",
    )(*args)


def _kv_kernel(h_ref, g_ref, wdkv_ref, ln_ref, tab_ref, wuk_ref, wuv_ref,
               c_ref, kr_ref, kn_ref, krkr_ref, v_ref):
    hn = _rms(h_ref[...], g_ref[...]).astype(BF16)
    ckr = jnp.dot(hn, wdkv_ref[...], preferred_element_type=F32)
    c = _rms(ckr[:, :KV_LORA_RANK], ln_ref[...])
    krkr = (ckr[:, KV_LORA_RANK:KV_LORA_RANK + LANES] * tab_ref[:, :LANES]
            + ckr[:, KV_LORA_RANK + LANES:] * tab_ref[:, LANES:])
    c_ref[...] = c
    kr_ref[...] = krkr[:, :QK_ROPE_DIM]
    krkr_ref[...] = krkr.astype(BF16)
    cb = c.astype(BF16)
    kn_ref[...] = jnp.dot(cb, wuk_ref[...], preferred_element_type=F32).astype(BF16)
    v_ref[...] = jnp.dot(cb, wuv_ref[...], preferred_element_type=F32).astype(BF16)


def _kv_layer(h, g, w_dkv_ext, lat_norm, tab, w_uk2, w_uv2):
    t = h.shape[0]

    def tile(w):
        return pl.BlockSpec((TOKEN_TILE, w), lambda i: (i, 0))

    hv = N_HEADS * V_HEAD_DIM
    hk = N_HEADS * QK_NOPE_DIM
    return pl.pallas_call(
        _kv_kernel,
        grid=(t // TOKEN_TILE,),
        in_specs=[tile(D_MODEL), _const_spec((1, D_MODEL)), _const_spec(w_dkv_ext.shape),
                  _const_spec((1, KV_LORA_RANK)), tile(2 * LANES), _const_spec(w_uk2.shape),
                  _const_spec(w_uv2.shape)],
        out_specs=[tile(KV_LORA_RANK), tile(QK_ROPE_DIM), tile(hk), tile(LANES), tile(hv)],
        out_shape=[jax.ShapeDtypeStruct((t, KV_LORA_RANK), F32),
                   jax.ShapeDtypeStruct((t, QK_ROPE_DIM), F32),
                   jax.ShapeDtypeStruct((t, hk), BF16),
                   jax.ShapeDtypeStruct((t, LANES), BF16),
                   jax.ShapeDtypeStruct((t, hv), BF16)],
        compiler_params=_params(),
        name="shared_kv",
    )(h, g, w_dkv_ext, lat_norm, tab, w_uk2, w_uv2)


def _q_kernel(h_ref, g_ref, wdq_ref, qn_ref, wuq_ref, tab_ref, wukt_ref,
              q_ref, qlat_ref, qrope_ref, *, n_sample):
    hn = _rms(h_ref[...], g_ref[...]).astype(BF16)
    cq = _rms(jnp.dot(hn, wdq_ref[...], preferred_element_type=F32), qn_ref[...]).astype(BF16)
    z = jnp.dot(cq, wuq_ref[...], preferred_element_type=F32) * ATTN_SCALE
    cos, sin = tab_ref[:, :LANES], tab_ref[:, LANES:]
    nope_w = N_HEADS * QK_NOPE_DIM
    rope_w = N_HEADS * QK_ROPE_DIM
    low = lax.broadcasted_iota(jnp.int32, (1, LANES), 1) < QK_ROPE_DIM
    ropes = []
    for t in range(N_HEADS // 2):
        rt = (z[:, nope_w + t * LANES:nope_w + (t + 1) * LANES] * cos
              + z[:, nope_w + rope_w + t * LANES:nope_w + rope_w + (t + 1) * LANES] * sin)
        ropes.append(rt)
        for half in range(2):
            hd = 2 * t + half
            base = hd * HEAD_DIM_PAD
            q_ref[:, base:base + LANES] = z[:, hd * LANES:(hd + 1) * LANES].astype(BF16)
            keep = low if half == 0 else jnp.logical_not(low)
            q_ref[:, base + LANES:base + 2 * LANES] = jnp.where(keep, rt, 0.0).astype(BF16)

    @pl.when(pl.program_id(0) == pl.num_programs(0) - 1)
    def _():
        rows = slice(TOKEN_TILE - n_sample, TOKEN_TILE)
        for hd in range(N_HEADS):
            qn = z[rows, hd * LANES:(hd + 1) * LANES].astype(BF16)
            qlat_ref[:, hd * KV_LORA_RANK:(hd + 1) * KV_LORA_RANK] = jnp.dot(
                qn, wukt_ref[hd], preferred_element_type=F32).astype(BF16)
        for t in range(N_HEADS // 2):
            qrope_ref[:, t * LANES:(t + 1) * LANES] = ropes[t][rows].astype(BF16)


def _q_layer(h, g, w_dq, q_norm, w_uq_ext, tab, w_ukt, n_sample):
    t = h.shape[0]

    def tile(w):
        return pl.BlockSpec((TOKEN_TILE, w), lambda i: (i, 0))

    return pl.pallas_call(
        functools.partial(_q_kernel, n_sample=n_sample),
        grid=(t // TOKEN_TILE,),
        in_specs=[tile(D_MODEL), _const_spec((1, D_MODEL)), _const_spec(w_dq.shape),
                  _const_spec((1, Q_LORA_RANK)), _const_spec(w_uq_ext.shape), tile(2 * LANES),
                  _const_spec(w_ukt.shape)],
        out_specs=[tile(N_HEADS * HEAD_DIM_PAD),
                   pl.BlockSpec((n_sample, N_HEADS * KV_LORA_RANK), lambda i: (0, 0)),
                   pl.BlockSpec((n_sample, N_HEADS * QK_ROPE_DIM), lambda i: (0, 0))],
        out_shape=[jax.ShapeDtypeStruct((t, N_HEADS * HEAD_DIM_PAD), BF16),
                   jax.ShapeDtypeStruct((n_sample, N_HEADS * KV_LORA_RANK), BF16),
                   jax.ShapeDtypeStruct((n_sample, N_HEADS * QK_ROPE_DIM), BF16)],
        compiler_params=_params(),
        name="mla_query",
    )(h, g, w_dq, q_norm, w_uq_ext, tab, w_ukt)


def _prompt_attn_kernel(q_ref, kn_ref, krkr_ref, v_ref, o_ref):
    seq = q_ref.shape[0]
    k = jnp.concatenate([kn_ref[...], krkr_ref[...]], axis=1)
    v = v_ref[...]
    row = lax.broadcasted_iota(jnp.int32, (Q_TILE, Q_TILE), 0)
    col = lax.broadcasted_iota(jnp.int32, (Q_TILE, Q_TILE), 1)
    for qi in range(seq // Q_TILE):
        kend = (qi + 1) * Q_TILE
        q = q_ref[qi * Q_TILE:kend, :]
        s = lax.dot_general(q, k[:kend], (((1,), (1,)), ((), ())), preferred_element_type=F32)
        diag = jnp.where(col <= row, s[:, kend - Q_TILE:], -jnp.inf)
        s = diag if qi == 0 else jnp.concatenate([s[:, :kend - Q_TILE], diag], axis=1)
        p = jnp.exp(s - jnp.max(s, axis=1, keepdims=True))
        denom = jnp.sum(p, axis=1, keepdims=True)
        o = jnp.dot(p.astype(BF16), v[:kend], preferred_element_type=F32)
        o_ref[qi * Q_TILE:kend, :] = (o / denom).astype(BF16)


def _prompt_attn(q_all, kn, krkr, v, batch, seq):
    t = q_all.shape[0]
    return pl.pallas_call(
        _prompt_attn_kernel,
        grid=(batch, N_HEADS),
        in_specs=[pl.BlockSpec((seq, HEAD_DIM_PAD), lambda b, h: (b, h)),
                  pl.BlockSpec((seq, QK_NOPE_DIM), lambda b, h: (b, h)),
                  pl.BlockSpec((seq, LANES), lambda b, h: (b, 0)),
                  pl.BlockSpec((seq, V_HEAD_DIM), lambda b, h: (b, h))],
        out_specs=pl.BlockSpec((seq, V_HEAD_DIM), lambda b, h: (b, h)),
        out_shape=jax.ShapeDtypeStruct((t, N_HEADS * V_HEAD_DIM), BF16),
        compiler_params=_params(n=2),
        name="prompt_attention",
    )(q_all, kn, krkr, v)


def _sample_attn_kernel(pt_ref, qlat_ref, qrope_ref, cnew_ref, krnew_ref, cc_ref, ckr_ref,
                        o_ref, cbuf, krbuf, sems, *, n_pages, page_size):
    s_idx = pl.program_id(0)
    n_seq = pl.num_programs(0)
    slot = s_idx % 2

    def copies(seq, sl, p):
        page = pt_ref[seq, p]
        keys = pl.ds(p * page_size, page_size)
        return (pltpu.make_async_copy(cc_ref.at[page], cbuf.at[sl, keys], sems.at[0, sl]),
                pltpu.make_async_copy(ckr_ref.at[page], krbuf.at[sl, :, keys], sems.at[1, sl]))

    def start_all(seq, sl):
        def body(p, carry):
            for cp in copies(seq, sl, p):
                cp.start()
            return carry
        lax.fori_loop(0, n_pages, body, 0)

    @pl.when(s_idx == 0)
    def _():
        start_all(0, 0)

    @pl.when(s_idx + 1 < n_seq)
    def _():
        start_all(s_idx + 1, 1 - slot)

    def wait_body(p, carry):
        for cp in copies(s_idx, slot, p):
            cp.wait()
        return carry
    lax.fori_loop(0, n_pages, wait_body, 0)

    q_lat = qlat_ref[0]
    q_rope = qrope_ref[0]
    c_new = cnew_ref[0]
    kr_new = krnew_ref[0]
    cb = cbuf[slot].astype(BF16)
    krt = krbuf[slot].astype(BF16)
    s = (lax.dot_general(q_lat, cb, (((1,), (1,)), ((), ())), preferred_element_type=F32)
         + jnp.dot(q_rope, krt, preferred_element_type=F32))
    s_new = (jnp.sum(q_lat.astype(F32) * c_new, axis=1, keepdims=True)
             + jnp.sum(q_rope.astype(F32) * kr_new, axis=1, keepdims=True))
    m = jnp.maximum(s_new, jnp.max(s, axis=1, keepdims=True))
    p = jnp.exp(s - m)
    p_new = jnp.exp(s_new - m)
    denom = p_new + jnp.sum(p, axis=1, keepdims=True)
    acc = p_new * c_new + jnp.dot(p.astype(BF16), cb, preferred_element_type=F32)
    o_ref[0] = (acc / denom).astype(BF16)


def _sample_attn(page_table, q_lat, q_rope, c_new, kr_new, cache_c, cache_krt):
    n_seq, n_pages = page_table.shape
    page_size = cache_c.shape[1]
    past = n_pages * page_size
    grid_spec = pltpu.PrefetchScalarGridSpec(
        num_scalar_prefetch=1,
        grid=(n_seq,),
        in_specs=[pl.BlockSpec((1, N_HEADS, KV_LORA_RANK), lambda s, pt: (s, 0, 0)),
                  pl.BlockSpec((1, N_HEADS, QK_ROPE_DIM), lambda s, pt: (s, 0, 0)),
                  pl.BlockSpec((1, 1, KV_LORA_RANK), lambda s, pt: (s, 0, 0)),
                  pl.BlockSpec((1, 1, QK_ROPE_DIM), lambda s, pt: (s, 0, 0)),
                  pl.BlockSpec(memory_space=pl.ANY),
                  pl.BlockSpec(memory_space=pl.ANY)],
        out_specs=pl.BlockSpec((1, N_HEADS, KV_LORA_RANK), lambda s, pt: (s, 0, 0)),
        scratch_shapes=[pltpu.VMEM((2, past, KV_LORA_RANK), F32),
                        pltpu.VMEM((2, QK_ROPE_DIM, past), F32),
                        pltpu.SemaphoreType.DMA((2, 2))],
    )
    return pl.pallas_call(
        functools.partial(_sample_attn_kernel, n_pages=n_pages, page_size=page_size),
        grid_spec=grid_spec,
        out_shape=jax.ShapeDtypeStruct((n_seq, N_HEADS, KV_LORA_RANK), BF16),
        compiler_params=_params(),
        name="sample_attention",
    )(page_table, q_lat, q_rope, c_new, kr_new, cache_c, cache_krt)


def _sample_out_kernel(olat_ref, wuv_ref, o_in_ref, o_ref):
    del o_in_ref
    for hd in range(N_HEADS):
        o_ref[:, hd * V_HEAD_DIM:(hd + 1) * V_HEAD_DIM] = jnp.dot(
            olat_ref[:, hd * KV_LORA_RANK:(hd + 1) * KV_LORA_RANK], wuv_ref[hd],
            preferred_element_type=F32).astype(BF16)


def _sample_out(o_lat, w_uvt, o_all):
    n_seq = o_lat.shape[0]
    t = o_all.shape[0]
    assert (t - n_seq) % n_seq == 0
    out_block = (t - n_seq) // n_seq
    return pl.pallas_call(
        _sample_out_kernel,
        grid=(1,),
        in_specs=[pl.BlockSpec(o_lat.shape, lambda i: (0, 0)),
                  pl.BlockSpec(w_uvt.shape, lambda i: (0, 0, 0)),
                  pl.BlockSpec(memory_space=pl.ANY)],
        out_specs=pl.BlockSpec((n_seq, N_HEADS * V_HEAD_DIM), lambda i: (out_block, 0)),
        out_shape=jax.ShapeDtypeStruct(o_all.shape, o_all.dtype),
        input_output_aliases={2: 0},
        compiler_params=_params(),
        name="sample_out",
    )(o_lat, w_uvt, o_all)


def _rope_table(seq, batch, n_sample, sample_pos):
    inv = ROPE_THETA ** (-jnp.arange(0, QK_ROPE_DIM, 2, dtype=F32) / QK_ROPE_DIM)
    pos = jnp.concatenate([jnp.tile(jnp.arange(seq, dtype=F32), batch),
                           jnp.full((n_sample,), sample_pos, F32)])
    ang = pos[:, None] * inv[None, :]
    cos, sin = jnp.cos(ang), jnp.sin(ang)
    reps = LANES // QK_ROPE_DIM
    return jnp.concatenate([cos, cos] * reps + [-sin, sin] * reps, axis=1)


def kernel(x_prompt, x_sample, cache_kv_latent, cache_k_rope, page_table, norm_mix, sgu_w_in,
           sgu_v_norm, sgu_w_s, sgu_b_s, sgu_w_out, norm_ffn, ffn_w_in, ffn_w_out, kv_norm, w_dkv,
           kv_latent_norm, w_uk, w_uv, w_dq, q_norm, w_uq, w_o, final_norm):
    batch, seq, _ = x_prompt.shape
    n_sample, dec_seq, _ = x_sample.shape
    assert dec_seq == 1 and seq % CHUNK == 0 and n_sample == CHUNK
    n_a = sgu_w_in.shape[0]
    n_b = w_dq.shape[0]
    n_pages = page_table.shape[1]
    page_size = cache_kv_latent.shape[1]
    past_len = n_pages * page_size
    t_prompt = batch * seq
    t = t_prompt + n_sample
    assert t % TOKEN_TILE == 0 and t_prompt % CHUNK == 0

    bf = lambda a: a.astype(BF16)
    row = lambda a: a.reshape(1, -1)
    rot = jnp.concatenate([jnp.arange(QK_ROPE_DIM // 2, QK_ROPE_DIM),
                           jnp.arange(0, QK_ROPE_DIM // 2)])
    w_kr = w_dkv[:, KV_LORA_RANK:]
    w_dkv_ext = bf(jnp.concatenate([w_dkv[:, :KV_LORA_RANK], w_kr, w_kr,
                                    w_kr[:, rot], w_kr[:, rot]], axis=1))
    w_uk2 = bf(w_uk.reshape(KV_LORA_RANK, N_HEADS * QK_NOPE_DIM))
    w_uv2 = bf(w_uv.reshape(KV_LORA_RANK, N_HEADS * V_HEAD_DIM))
    w_ukt = bf(jnp.transpose(w_uk, (1, 2, 0)))
    w_uvt = bf(jnp.transpose(w_uv, (1, 0, 2)))
    w_uq3 = w_uq.reshape(n_b, Q_LORA_RANK, N_HEADS, QK_NOPE_DIM + QK_ROPE_DIM)
    w_uq_rope = w_uq3[..., QK_NOPE_DIM:]
    w_uq_ext = bf(jnp.concatenate(
        [w_uq3[..., :QK_NOPE_DIM].reshape(n_b, Q_LORA_RANK, -1),
         w_uq_rope.reshape(n_b, Q_LORA_RANK, -1),
         w_uq_rope[..., rot].reshape(n_b, Q_LORA_RANK, -1)], axis=2))
    tab = _rope_table(seq, batch, n_sample, float(past_len))
    cache_krt = jnp.swapaxes(cache_k_rope, 1, 2)

    h = jnp.concatenate([x_prompt.reshape(t_prompt, D_MODEL), x_sample.reshape(n_sample, D_MODEL)])
    v_rows = []
    for l in range(n_a):
        h, v_s = _sgu_layer(h, row(norm_mix[l]), bf(sgu_w_in[l]), row(sgu_v_norm[l]), sgu_w_s[l],
                            sgu_b_s[l].T, bf(sgu_w_out[l]), n_sample)
        v_rows.append(v_s)
        h = _ffn_layer(h, row(norm_ffn[l]), bf(ffn_w_in[l]), bf(ffn_w_out[l]))

    c_all, kr_all, kn, krkr, v = _kv_layer(h, row(kv_norm), w_dkv_ext, row(kv_latent_norm), tab,
                                           w_uk2, w_uv2)
    c_new = c_all[t_prompt:].reshape(n_sample, 1, KV_LORA_RANK)
    kr_new = kr_all[t_prompt:].reshape(n_sample, 1, QK_ROPE_DIM)

    for j in range(n_b):
        l = n_a + j
        q_all, q_lat, q_rope = _q_layer(h, row(norm_mix[l]), bf(w_dq[j]), row(q_norm[j]),
                                        w_uq_ext[j], tab, w_ukt, n_sample)
        o = _prompt_attn(q_all, kn, krkr, v, batch, seq)
        o_lat = _sample_attn(page_table, q_lat.reshape(n_sample, N_HEADS, KV_LORA_RANK),
                             q_rope.reshape(n_sample, N_HEADS, QK_ROPE_DIM), c_new, kr_new,
                             cache_kv_latent, cache_krt)
        o = _sample_out(o_lat.reshape(n_sample, N_HEADS * KV_LORA_RANK), w_uvt, o)
        h = _ffn_layer(h, row(norm_ffn[l]), bf(ffn_w_in[l]), bf(ffn_w_out[l]),
                       attn=(o, bf(w_o[j])), final=row(final_norm) if j == n_b - 1 else None)

    return (h[:t_prompt].reshape(batch, seq, D_MODEL),
            h[t_prompt:].reshape(n_sample, 1, D_MODEL),
            c_all[:t_prompt].reshape(batch, seq, KV_LORA_RANK),
            kr_all[:t_prompt].reshape(batch, seq, QK_ROPE_DIM),
            c_all[t_prompt:].reshape(n_sample, 1, KV_LORA_RANK),
            kr_all[t_prompt:].reshape(n_sample, 1, QK_ROPE_DIM),
            jnp.stack(v_rows).reshape(n_a, n_sample, 1, D_SGU))
```

```python
import functools

import jax
import jax.numpy as jnp
from jax import lax
from jax.experimental import pallas as pl
from jax.experimental.pallas import tpu as pltpu

D_MODEL = 1024
CHUNK = 128
D_SGU = 2 * D_MODEL
SGU_GROUPS = 8
SGU_GROUP_DIM = D_SGU // SGU_GROUPS
N_HEADS = 16
QK_NOPE_DIM = 128
QK_ROPE_DIM = 64
V_HEAD_DIM = 128
Q_LORA_RANK = 384
KV_LORA_RANK = 256
ROPE_THETA = 10000.0
ATTN_SCALE = (QK_NOPE_DIM + QK_ROPE_DIM) ** -0.5
EPS = 1e-6

LANES = 128
TOKEN_TILE = 512
HEAD_DIM_PAD = 2 * LANES
Q_TILE = 256
KEY_CHUNK = 2048
VMEM_LIMIT = 56 * 1024 * 1024

F32 = jnp.float32
BF16 = jnp.bfloat16


def _rms(x, g):
    return x * lax.rsqrt(jnp.mean(x * x, axis=-1, keepdims=True) + EPS) * g


def _const_spec(shape):
    zeros = (0,) * len(shape)
    return pl.BlockSpec(shape, lambda *_: zeros, pipeline_mode=pl.Buffered(1))


def _params(n=1):
    return pltpu.CompilerParams(dimension_semantics=("arbitrary",) * n,
                                vmem_limit_bytes=VMEM_LIMIT)


def _two_stream_call(body, prompt_in, sample_in, consts, prompt_out, sample_out, *, name,
                     scratch=(), setup=None):
    t_prompt = prompt_in[0].shape[0]
    n_sample = sample_in[0].shape[0]
    n_tiles = t_prompt // TOKEN_TILE
    assert t_prompt % TOKEN_TILE == 0
    n_in, n_c = len(prompt_in), len(consts)
    prompt_out = [po for po in prompt_out if po is not None]
    sample_out = [so for so in sample_out if so is not None]
    n_po, n_so = len(prompt_out), len(sample_out)

    def kern(*refs):
        refs = list(refs)
        p_in, refs = refs[:n_in], refs[n_in:]
        s_in, refs = refs[:n_in], refs[n_in:]
        c_in, refs = refs[:n_c], refs[n_c:]
        p_out, refs = refs[:n_po], refs[n_po:]
        s_out, scr = refs[:n_so], refs[n_so:]
        i = pl.program_id(0)
        if setup is not None:
            pl.when(i == 0)(lambda: setup(c_in, scr))
        pl.when(i < n_tiles)(lambda: body(p_in, c_in, p_out, scr, False))
        pl.when(i == n_tiles)(lambda: body(s_in, c_in, s_out, scr, True))

    def p_spec(width, rows=None):
        if rows is None:
            return pl.BlockSpec((TOKEN_TILE, width), lambda i: (jnp.minimum(i, n_tiles - 1), 0))
        period = rows // TOKEN_TILE
        return pl.BlockSpec((TOKEN_TILE, width),
                            lambda i: (jnp.minimum(i, n_tiles - 1) % period, 0))

    def s_spec(rows, width):
        return pl.BlockSpec((rows, width), lambda i: (0, 0))

    in_specs = ([p_spec(a.shape[1], None if a.shape[0] == t_prompt else a.shape[0])
                 for a in prompt_in]
                + [s_spec(*a.shape) for a in sample_in]
                + [_const_spec(a.shape) for a in consts])
    out_specs = ([p_spec(w) for w, _ in prompt_out] + [s_spec(n_sample, w) for w, _ in sample_out])
    out_shape = ([jax.ShapeDtypeStruct((t_prompt, w), d) for w, d in prompt_out]
                 + [jax.ShapeDtypeStruct((n_sample, w), d) for w, d in sample_out])
    return pl.pallas_call(
        kern, grid=(n_tiles + 1,), in_specs=in_specs, out_specs=out_specs, out_shape=out_shape,
        scratch_shapes=list(scratch), compiler_params=_params(), name=name,
    )(*prompt_in, *sample_in, *consts)


def _sgu_setup(consts, scratch):
    ws_ref, bt_ref = consts[3], consts[4]
    wmix_ref, bmix_ref = scratch
    r = lax.broadcasted_iota(jnp.int32, (CHUNK, CHUNK), 0)
    c = lax.broadcasted_iota(jnp.int32, (CHUNK, CHUNK), 1)
    for g in range(SGU_GROUPS):
        w = ws_ref[g]
        wmix_ref[0, g] = jnp.where(c <= r, w, 0.0).astype(BF16)
        wmix_ref[1, g] = jnp.where(c == r, w[0:1, 0:1], 0.0).astype(BF16)
    bmix_ref[0] = bt_ref[...]
    bmix_ref[1] = jnp.broadcast_to(bt_ref[0:1, :], bt_ref.shape)


def _sgu_body(ins, consts, outs, scratch, is_sample):
    (h_ref,) = ins
    g_ref, win_ref, vn_ref, _, _, wout_ref = consts
    wmix_ref, bmix_ref = scratch
    sel = 1 if is_sample else 0
    x = h_ref[...]
    hn = _rms(x, g_ref[...]).astype(BF16)
    z = jax.nn.gelu(jnp.dot(hn, win_ref[...], preferred_element_type=F32))
    u = z[:, :D_SGU]
    v = _rms(z[:, D_SGU:], vn_ref[...])
    if is_sample:
        outs[1][...] = v
    vb = v.astype(BF16)
    bias = bmix_ref[sel]
    ys = []
    for c in range(x.shape[0] // CHUNK):
        rows = slice(c * CHUNK, (c + 1) * CHUNK)
        parts = []
        for g in range(SGU_GROUPS):
            cols = slice(g * SGU_GROUP_DIM, (g + 1) * SGU_GROUP_DIM)
            m = jnp.dot(wmix_ref[sel, g], vb[rows, cols], preferred_element_type=F32)
            parts.append(m + bias[:, g:g + 1])
        ys.append((u[rows] * jnp.concatenate(parts, axis=1)).astype(BF16))
    y = ys[0] if len(ys) == 1 else jnp.concatenate(ys, axis=0)
    outs[0][...] = x + jnp.dot(y, wout_ref[...], preferred_element_type=F32)


def _sgu_layer(hp, hs, g, w_in, v_norm, w_s, b_t, w_out):
    return _two_stream_call(
        _sgu_body, [hp], [hs], [g, w_in, v_norm, w_s, b_t, w_out],
        [(D_MODEL, F32), None], [(D_MODEL, F32), (D_SGU, F32)], name="sgu_mixer", setup=_sgu_setup,
        scratch=[pltpu.VMEM((2, SGU_GROUPS, CHUNK, CHUNK), BF16),
                 pltpu.VMEM((2, CHUNK, SGU_GROUPS), F32)])


def _ffn_body(ins, consts, outs, scratch, is_sample, *, d_ff, n_split, has_attn, has_final):
    del scratch
    consts = list(consts)
    x = ins[0][...]
    if has_attn:
        wuv_ref, wo_ref = consts.pop(0), consts.pop(0)
        if is_sample:
            o = jnp.concatenate(
                [jnp.dot(ins[1][:, hd * KV_LORA_RANK:(hd + 1) * KV_LORA_RANK], wuv_ref[hd],
                         preferred_element_type=F32).astype(BF16) for hd in range(N_HEADS)], axis=1)
        else:
            o = ins[1][...]
        x = x + jnp.dot(o, wo_ref[...], preferred_element_type=F32)
    g_ref, win_ref, wout_ref = consts[:3]
    hn = _rms(x, g_ref[...]).astype(BF16)
    acc = x
    width = d_ff // n_split
    for j in range(n_split):
        gate = jnp.dot(hn, win_ref[:, j * width:(j + 1) * width], preferred_element_type=F32)
        up = jnp.dot(hn, win_ref[:, d_ff + j * width:d_ff + (j + 1) * width],
                     preferred_element_type=F32)
        a = (jax.nn.silu(gate) * up).astype(BF16)
        acc = acc + jnp.dot(a, wout_ref[j * width:(j + 1) * width, :], preferred_element_type=F32)
    if has_final:
        acc = _rms(acc, consts[3][...])
    outs[0][...] = acc


def _ffn_layer(hp, hs, g, w_in, w_out, attn=None, final=None):
    prompt_in, sample_in, consts = [hp], [hs], []
    if attn is not None:
        o_prompt, o_lat_sample, w_uvt, w_o = attn
        prompt_in.append(o_prompt)
        sample_in.append(o_lat_sample)
        consts += [w_uvt, w_o]
    consts += [g, w_in, w_out]
    if final is not None:
        consts.append(final)
    body = functools.partial(_ffn_body, d_ff=w_out.shape[0], n_split=2, has_attn=attn is not None,
                             has_final=final is not None)
    return _two_stream_call(body, prompt_in, sample_in, consts, [(D_MODEL, F32)], [(D_MODEL, F32)],
                            name="ffn")


def _kv_body(ins, consts, outs, scratch, is_sample):
    del scratch
    h_ref, tab_ref = ins
    g_ref, wdkv_ref, ln_ref, wuk_ref, wuv_ref = consts
    hn = _rms(h_ref[...], g_ref[...]).astype(BF16)
    ckr = jnp.dot(hn, wdkv_ref[...], preferred_element_type=F32)
    c = _rms(ckr[:, :KV_LORA_RANK], ln_ref[...])
    krkr = (ckr[:, KV_LORA_RANK:KV_LORA_RANK + LANES] * tab_ref[:, :LANES]
            + ckr[:, KV_LORA_RANK + LANES:] * tab_ref[:, LANES:])
    outs[0][...] = c
    outs[1][...] = krkr[:, :QK_ROPE_DIM]
    if not is_sample:
        cb = c.astype(BF16)
        outs[2][...] = jnp.dot(cb, wuk_ref[...], preferred_element_type=F32).astype(BF16)
        outs[3][...] = krkr.astype(BF16)
        outs[4][...] = jnp.dot(cb, wuv_ref[...], preferred_element_type=F32).astype(BF16)


def _kv_layer(hp, hs, tab_p, tab_s, g, w_dkv_ext, lat_norm, w_uk2, w_uv2):
    return _two_stream_call(
        _kv_body, [hp, tab_p], [hs, tab_s], [g, w_dkv_ext, lat_norm, w_uk2, w_uv2],
        [(KV_LORA_RANK, F32), (QK_ROPE_DIM, F32), (N_HEADS * QK_NOPE_DIM, BF16), (LANES, BF16),
         (N_HEADS * V_HEAD_DIM, BF16)],
        [(KV_LORA_RANK, F32), (QK_ROPE_DIM, F32), None, None, None], name="shared_kv")


def _q_body(ins, consts, outs, scratch, is_sample):
    del scratch
    h_ref, tab_ref = ins
    g_ref, wdq_ref, qn_ref, wuq_ref, wukt_ref = consts
    hn = _rms(h_ref[...], g_ref[...]).astype(BF16)
    cq = _rms(jnp.dot(hn, wdq_ref[...], preferred_element_type=F32), qn_ref[...]).astype(BF16)
    z = jnp.dot(cq, wuq_ref[...], preferred_element_type=F32) * ATTN_SCALE
    cos, sin = tab_ref[:, :LANES], tab_ref[:, LANES:]
    nope_w = N_HEADS * QK_NOPE_DIM
    rope_w = N_HEADS * QK_ROPE_DIM
    low = lax.broadcasted_iota(jnp.int32, (1, LANES), 1) < QK_ROPE_DIM
    for t in range(N_HEADS // 2):
        rt = (z[:, nope_w + t * LANES:nope_w + (t + 1) * LANES] * cos
              + z[:, nope_w + rope_w + t * LANES:nope_w + rope_w + (t + 1) * LANES] * sin)
        if is_sample:
            outs[1][:, t * LANES:(t + 1) * LANES] = rt.astype(BF16)
            continue
        for half in range(2):
            hd = 2 * t + half
            base = hd * HEAD_DIM_PAD
            outs[0][:, base:base + LANES] = z[:, hd * LANES:(hd + 1) * LANES].astype(BF16)
            keep = low if half == 0 else jnp.logical_not(low)
            outs[0][:, base + LANES:base + 2 * LANES] = jnp.where(keep, rt, 0.0).astype(BF16)
    if is_sample:
        for hd in range(N_HEADS):
            qn = z[:, hd * LANES:(hd + 1) * LANES].astype(BF16)
            outs[0][:, hd * KV_LORA_RANK:(hd + 1) * KV_LORA_RANK] = jnp.dot(
                qn, wukt_ref[hd], preferred_element_type=F32).astype(BF16)


def _q_layer(hp, hs, tab_p, tab_s, g, w_dq, q_norm, w_uq_ext, w_ukt):
    return _two_stream_call(
        _q_body, [hp, tab_p], [hs, tab_s], [g, w_dq, q_norm, w_uq_ext, w_ukt],
        [(N_HEADS * HEAD_DIM_PAD, BF16), None],
        [(N_HEADS * KV_LORA_RANK, BF16), (N_HEADS * QK_ROPE_DIM, BF16)], name="mla_query")


def _attn_kernel(pt_ref, q_ref, kn_ref, krkr_ref, v_ref, qlat_ref, qrope_ref, cnew_ref, krnew_ref,
                 cc_ref, ckr_ref, o_ref, olat_ref, cbuf, krbuf, sems, *, n_pages, page_size):
    n_seq = pl.num_programs(0) * pl.num_programs(1)
    s_idx = pl.program_id(0) * pl.num_programs(1) + pl.program_id(1)
    slot = s_idx % 2

    def start_pages(seq, sl):
        for p in range(n_pages):
            page = pt_ref[seq, p]
            keys = pl.ds(p * page_size, page_size)
            pltpu.make_async_copy(cc_ref.at[page], cbuf.at[sl, keys], sems.at[0, sl]).start()
            pltpu.make_async_copy(ckr_ref.at[page], krbuf.at[sl, :, keys], sems.at[1, sl]).start()

    def wait_pages(sl):
        pltpu.make_async_copy(cbuf.at[sl], cbuf.at[sl], sems.at[0, sl]).wait()
        pltpu.make_async_copy(krbuf.at[sl], krbuf.at[sl], sems.at[1, sl]).wait()

    @pl.when(s_idx == 0)
    def _():
        start_pages(0, 0)

    wait_pages(slot)
    start_pages(jnp.minimum(s_idx + 1, n_seq - 1), 1 - slot)

    seq = q_ref.shape[0]
    k = jnp.concatenate([kn_ref[...], krkr_ref[...]], axis=1)
    v = v_ref[...]
    row = lax.broadcasted_iota(jnp.int32, (Q_TILE, Q_TILE), 0)
    col = lax.broadcasted_iota(jnp.int32, (Q_TILE, Q_TILE), 1)
    nt = (((1,), (1,)), ((), ()))
    for qi in range(seq // Q_TILE):
        kend = (qi + 1) * Q_TILE
        q = q_ref[qi * Q_TILE:kend, :]
        s = lax.dot_general(q, k[:kend], nt, preferred_element_type=F32)
        diag = jnp.where(col <= row, s[:, kend - Q_TILE:], -jnp.inf)
        s = diag if qi == 0 else jnp.concatenate([s[:, :kend - Q_TILE], diag], axis=1)
        p = jnp.exp(s - jnp.max(s, axis=1, keepdims=True))
        denom = jnp.sum(p, axis=1, keepdims=True)
        o = jnp.dot(p.astype(BF16), v[:kend], preferred_element_type=F32)
        o_ref[qi * Q_TILE:kend, :] = (o / denom).astype(BF16)

    q_lat = qlat_ref[0]
    q_rope = qrope_ref[0]
    c_new = cnew_ref[0]
    kr_new = krnew_ref[0]
    m = (jnp.sum(q_lat.astype(F32) * c_new, axis=1, keepdims=True)
         + jnp.sum(q_rope.astype(F32) * kr_new, axis=1, keepdims=True))
    denom = jnp.ones_like(m)
    acc = jnp.broadcast_to(c_new, (N_HEADS, KV_LORA_RANK))
    for j in range(n_pages * page_size // KEY_CHUNK):
        keys = slice(j * KEY_CHUNK, (j + 1) * KEY_CHUNK)
        cb = cbuf[slot, keys, :].astype(BF16)
        krt = krbuf[slot, :, keys].astype(BF16)
        s = (lax.dot_general(q_lat, cb, nt, preferred_element_type=F32)
             + jnp.dot(q_rope, krt, preferred_element_type=F32))
        m_new = jnp.maximum(m, jnp.max(s, axis=1, keepdims=True))
        alpha = jnp.exp(m - m_new)
        p = jnp.exp(s - m_new)
        denom = denom * alpha + jnp.sum(p, axis=1, keepdims=True)
        acc = acc * alpha + jnp.dot(p.astype(BF16), cb, preferred_element_type=F32)
        m = m_new
    olat_ref[0] = (acc / denom).astype(BF16)

    @pl.when(s_idx == n_seq - 1)
    def _():
        wait_pages(1 - slot)


def _attention(page_table, q_all, kn, krkr, v, q_lat, q_rope, c_new, kr_new, cache_c, cache_krt,
               batch, seq):
    n_seq, n_pages = page_table.shape
    page_size = cache_c.shape[1]
    past = n_pages * page_size
    assert n_seq == batch * N_HEADS and past % KEY_CHUNK == 0 and seq % Q_TILE == 0

    def per_seq(width):
        return pl.BlockSpec((1,) + width, lambda b, h, pt: (b * N_HEADS + h, 0, 0))

    grid_spec = pltpu.PrefetchScalarGridSpec(
        num_scalar_prefetch=1,
        grid=(batch, N_HEADS),
        in_specs=[pl.BlockSpec((seq, HEAD_DIM_PAD), lambda b, h, pt: (b, h)),
                  pl.BlockSpec((seq, QK_NOPE_DIM), lambda b, h, pt: (b, h)),
                  pl.BlockSpec((seq, LANES), lambda b, h, pt: (b, 0)),
                  pl.BlockSpec((seq, V_HEAD_DIM), lambda b, h, pt: (b, h)),
                  per_seq((N_HEADS, KV_LORA_RANK)), per_seq((N_HEADS, QK_ROPE_DIM)),
                  per_seq((1, KV_LORA_RANK)), per_seq((1, QK_ROPE_DIM)),
                  pl.BlockSpec(memory_space=pl.ANY), pl.BlockSpec(memory_space=pl.ANY)],
        out_specs=[pl.BlockSpec((seq, V_HEAD_DIM), lambda b, h, pt: (b, h)),
                   per_seq((N_HEADS, KV_LORA_RANK))],
        scratch_shapes=[pltpu.VMEM((2, past, KV_LORA_RANK), F32),
                        pltpu.VMEM((2, QK_ROPE_DIM, past), F32),
                        pltpu.SemaphoreType.DMA((2, 2))],
    )
    return pl.pallas_call(
        functools.partial(_attn_kernel, n_pages=n_pages, page_size=page_size),
        grid_spec=grid_spec,
        out_shape=[jax.ShapeDtypeStruct((batch * seq, N_HEADS * V_HEAD_DIM), BF16),
                   jax.ShapeDtypeStruct((n_seq, N_HEADS, KV_LORA_RANK), BF16)],
        compiler_params=_params(2),
        name="attention",
    )(page_table, q_all, kn, krkr, v, q_lat, q_rope, c_new, kr_new, cache_c, cache_krt)


def _rope_table(pos):
    inv = ROPE_THETA ** (-jnp.arange(0, QK_ROPE_DIM, 2, dtype=F32) / QK_ROPE_DIM)
    ang = pos[:, None] * inv[None, :]
    cos, sin = jnp.cos(ang), jnp.sin(ang)
    reps = LANES // QK_ROPE_DIM
    return jnp.concatenate([cos, cos] * reps + [-sin, sin] * reps, axis=1)


def kernel(x_prompt, x_sample, cache_kv_latent, cache_k_rope, page_table, norm_mix, sgu_w_in,
           sgu_v_norm, sgu_w_s, sgu_b_s, sgu_w_out, norm_ffn, ffn_w_in, ffn_w_out, kv_norm, w_dkv,
           kv_latent_norm, w_uk, w_uv, w_dq, q_norm, w_uq, w_o, final_norm):
    batch, seq, _ = x_prompt.shape
    n_sample, dec_seq, _ = x_sample.shape
    n_pages = page_table.shape[1]
    page_size = cache_kv_latent.shape[1]
    past_len = n_pages * page_size
    assert dec_seq == 1 and past_len % CHUNK == 0 and seq % TOKEN_TILE == 0
    assert n_sample == CHUNK
    n_a = sgu_w_in.shape[0]
    n_b = w_dq.shape[0]
    t_prompt = batch * seq

    bf = lambda a: a.astype(BF16)
    row = lambda a: a.reshape(1, -1)
    rot = jnp.concatenate([jnp.arange(QK_ROPE_DIM // 2, QK_ROPE_DIM),
                           jnp.arange(0, QK_ROPE_DIM // 2)])
    w_kr = w_dkv[:, KV_LORA_RANK:]
    w_dkv_ext = bf(jnp.concatenate([w_dkv[:, :KV_LORA_RANK], w_kr, w_kr,
                                    w_kr[:, rot], w_kr[:, rot]], axis=1))
    w_uk2 = bf(w_uk.reshape(KV_LORA_RANK, N_HEADS * QK_NOPE_DIM))
    w_uv2 = bf(w_uv.reshape(KV_LORA_RANK, N_HEADS * V_HEAD_DIM))
    w_ukt = bf(jnp.transpose(w_uk, (1, 2, 0)))
    w_uvt = bf(jnp.transpose(w_uv, (1, 0, 2)))
    w_uq3 = w_uq.reshape(n_b, Q_LORA_RANK, N_HEADS, QK_NOPE_DIM + QK_ROPE_DIM)
    w_uq_rope = w_uq3[..., QK_NOPE_DIM:]
    w_uq_ext = bf(jnp.concatenate(
        [w_uq3[..., :QK_NOPE_DIM].reshape(n_b, Q_LORA_RANK, -1),
         w_uq_rope.reshape(n_b, Q_LORA_RANK, -1),
         w_uq_rope[..., rot].reshape(n_b, Q_LORA_RANK, -1)], axis=2))
    tab_p = _rope_table(jnp.arange(seq, dtype=F32))
    tab_s = jnp.broadcast_to(_rope_table(jnp.full((1,), past_len, F32)), (n_sample, 2 * LANES))
    cache_krt = jnp.swapaxes(cache_k_rope, 1, 2)

    hp = x_prompt.reshape(t_prompt, D_MODEL)
    hs = x_sample.reshape(n_sample, D_MODEL)
    v_rows = []
    for l in range(n_a):
        hp, hs, v_s = _sgu_layer(hp, hs, row(norm_mix[l]), bf(sgu_w_in[l]), row(sgu_v_norm[l]),
                                 sgu_w_s[l], sgu_b_s[l].T, bf(sgu_w_out[l]))
        v_rows.append(v_s)
        hp, hs = _ffn_layer(hp, hs, row(norm_ffn[l]), bf(ffn_w_in[l]), bf(ffn_w_out[l]))

    c_p, kr_p, kn, krkr, v, c_s, kr_s = _kv_layer(hp, hs, tab_p, tab_s, row(kv_norm), w_dkv_ext,
                                                  row(kv_latent_norm), w_uk2, w_uv2)
    c_new = c_s.reshape(n_sample, 1, KV_LORA_RANK)
    kr_new = kr_s.reshape(n_sample, 1, QK_ROPE_DIM)

    for j in range(n_b):
        l = n_a + j
        q_all, q_lat, q_rope = _q_layer(hp, hs, tab_p, tab_s, row(norm_mix[l]), bf(w_dq[j]),
                                        row(q_norm[j]), w_uq_ext[j], w_ukt)
        o_p, o_lat = _attention(page_table, q_all, kn, krkr, v,
                                q_lat.reshape(n_sample, N_HEADS, KV_LORA_RANK),
                                q_rope.reshape(n_sample, N_HEADS, QK_ROPE_DIM), c_new, kr_new,
                                cache_kv_latent, cache_krt, batch, seq)
        hp, hs = _ffn_layer(hp, hs, row(norm_ffn[l]), bf(ffn_w_in[l]), bf(ffn_w_out[l]),
                            attn=(o_p, o_lat.reshape(n_sample, N_HEADS * KV_LORA_RANK), w_uvt,
                                  bf(w_o[j])),
                            final=row(final_norm) if j == n_b - 1 else None)

    return (hp.reshape(batch, seq, D_MODEL),
            hs.reshape(n_sample, 1, D_MODEL),
            c_p.reshape(batch, seq, KV_LORA_RANK),
            kr_p.reshape(batch, seq, QK_ROPE_DIM),
            c_new,
            kr_new,
            jnp.stack(v_rows).reshape(n_a, n_sample, 1, D_SGU))
```

```python
import functools

import jax
import jax.numpy as jnp
from jax import lax
from jax.experimental import pallas as pl
from jax.experimental.pallas import tpu as pltpu

D_MODEL = 1024
CHUNK = 128
D_SGU = 2 * D_MODEL
SGU_GROUPS = 8
SGU_GROUP_DIM = D_SGU // SGU_GROUPS
N_HEADS = 16
QK_NOPE_DIM = 128
QK_ROPE_DIM = 64
V_HEAD_DIM = 128
Q_LORA_RANK = 384
KV_LORA_RANK = 256
ROPE_THETA = 10000.0
ATTN_SCALE = (QK_NOPE_DIM + QK_ROPE_DIM) ** -0.5
EPS = 1e-6

LANES = 128
TOKEN_TILE = 512
HEAD_DIM_PAD = 2 * LANES
Q_TILE = 256
KEY_CHUNK = 2048
VMEM_LIMIT = 56 * 1024 * 1024

F32 = jnp.float32
BF16 = jnp.bfloat16


def _rms(x, g):
    return x * lax.rsqrt(jnp.mean(x * x, axis=-1, keepdims=True) + EPS) * g


def _const_spec(shape):
    zeros = (0,) * len(shape)
    return pl.BlockSpec(shape, lambda *_: zeros, pipeline_mode=pl.Buffered(1))


def _params(n=1):
    return pltpu.CompilerParams(dimension_semantics=("arbitrary",) * n,
                                vmem_limit_bytes=VMEM_LIMIT)


def _two_stream_call(body, prompt_in, sample_in, consts, prompt_out, sample_out, *, name,
                     scratch=(), setup=None):
    t_prompt = prompt_in[0].shape[0]
    n_sample = sample_in[0].shape[0]
    n_tiles = t_prompt // TOKEN_TILE
    assert t_prompt % TOKEN_TILE == 0
    n_in, n_c = len(prompt_in), len(consts)
    prompt_out = [po for po in prompt_out if po is not None]
    sample_out = [so for so in sample_out if so is not None]
    n_po, n_so = len(prompt_out), len(sample_out)

    def kern(*refs):
        refs = list(refs)
        p_in, refs = refs[:n_in], refs[n_in:]
        s_in, refs = refs[:n_in], refs[n_in:]
        c_in, refs = refs[:n_c], refs[n_c:]
        p_out, refs = refs[:n_po], refs[n_po:]
        s_out, scr = refs[:n_so], refs[n_so:]
        i = pl.program_id(0)
        if setup is not None:
            pl.when(i == 0)(lambda: setup(c_in, scr))
        pl.when(i < n_tiles)(lambda: body(p_in, c_in, p_out, scr, False))
        pl.when(i == n_tiles)(lambda: body(s_in, c_in, s_out, scr, True))

    def p_spec(width, rows=None):
        if rows is None:
            return pl.BlockSpec((TOKEN_TILE, width), lambda i: (jnp.minimum(i, n_tiles - 1), 0))
        period = rows // TOKEN_TILE
        return pl.BlockSpec((TOKEN_TILE, width),
                            lambda i: (jnp.minimum(i, n_tiles - 1) % period, 0))

    def s_spec(rows, width):
        return pl.BlockSpec((rows, width), lambda i: (0, 0))

    in_specs = ([p_spec(a.shape[1], None if a.shape[0] == t_prompt else a.shape[0])
                 for a in prompt_in]
                + [s_spec(*a.shape) for a in sample_in]
                + [_const_spec(a.shape) for a in consts])
    out_specs = ([p_spec(w) for w, _ in prompt_out] + [s_spec(n_sample, w) for w, _ in sample_out])
    out_shape = ([jax.ShapeDtypeStruct((t_prompt, w), d) for w, d in prompt_out]
                 + [jax.ShapeDtypeStruct((n_sample, w), d) for w, d in sample_out])
    return pl.pallas_call(
        kern, grid=(n_tiles + 1,), in_specs=in_specs, out_specs=out_specs, out_shape=out_shape,
        scratch_shapes=list(scratch), compiler_params=_params(), name=name,
    )(*prompt_in, *sample_in, *consts)


def _sgu_setup(consts, scratch):
    ws_ref, bt_ref = consts[3], consts[4]
    wmix_ref, bmix_ref = scratch
    r = lax.broadcasted_iota(jnp.int32, (CHUNK, CHUNK), 0)
    c = lax.broadcasted_iota(jnp.int32, (CHUNK, CHUNK), 1)
    for g in range(SGU_GROUPS):
        w = ws_ref[g]
        wmix_ref[0, g] = jnp.where(c <= r, w, 0.0).astype(BF16)
        wmix_ref[1, g] = jnp.where(c == r, w[0:1, 0:1], 0.0).astype(BF16)
    bmix_ref[0] = bt_ref[...]
    bmix_ref[1] = jnp.broadcast_to(bt_ref[0:1, :], bt_ref.shape)


def _sgu_body(ins, consts, outs, scratch, is_sample):
    (h_ref,) = ins
    g_ref, win_ref, vn_ref, _, _, wout_ref = consts
    wmix_ref, bmix_ref = scratch
    sel = 1 if is_sample else 0
    x = h_ref[...]
    hn = _rms(x, g_ref[...]).astype(BF16)
    z = jax.nn.gelu(jnp.dot(hn, win_ref[...], preferred_element_type=F32))
    u = z[:, :D_SGU]
    v = _rms(z[:, D_SGU:], vn_ref[...])
    if is_sample:
        outs[1][...] = v
    vb = v.astype(BF16)
    bias = bmix_ref[sel]
    ys = []
    for c in range(x.shape[0] // CHUNK):
        rows = slice(c * CHUNK, (c + 1) * CHUNK)
        parts = []
        for g in range(SGU_GROUPS):
            cols = slice(g * SGU_GROUP_DIM, (g + 1) * SGU_GROUP_DIM)
            m = jnp.dot(wmix_ref[sel, g], vb[rows, cols], preferred_element_type=F32)
            parts.append(m + bias[:, g:g + 1])
        ys.append((u[rows] * jnp.concatenate(parts, axis=1)).astype(BF16))
    y = ys[0] if len(ys) == 1 else jnp.concatenate(ys, axis=0)
    outs[0][...] = x + jnp.dot(y, wout_ref[...], preferred_element_type=F32)


def _sgu_layer(hp, hs, g, w_in, v_norm, w_s, b_t, w_out):
    return _two_stream_call(
        _sgu_body, [hp], [hs], [g, w_in, v_norm, w_s, b_t, w_out],
        [(D_MODEL, F32), None], [(D_MODEL, F32), (D_SGU, F32)], name="sgu_mixer", setup=_sgu_setup,
        scratch=[pltpu.VMEM((2, SGU_GROUPS, CHUNK, CHUNK), BF16),
                 pltpu.VMEM((2, CHUNK, SGU_GROUPS), F32)])


def _ffn_body(ins, consts, outs, scratch, is_sample, *, d_ff, n_split, has_attn, has_final):
    del scratch
    consts = list(consts)
    x = ins[0][...]
    if has_attn:
        wuv_ref, wo_ref = consts.pop(0), consts.pop(0)
        if is_sample:
            o = jnp.concatenate(
                [jnp.dot(ins[1][:, hd * KV_LORA_RANK:(hd + 1) * KV_LORA_RANK], wuv_ref[hd],
                         preferred_element_type=F32).astype(BF16) for hd in range(N_HEADS)], axis=1)
        else:
            o = ins[1][...]
        x = x + jnp.dot(o, wo_ref[...], preferred_element_type=F32)
    g_ref, win_ref, wout_ref = consts[:3]
    hn = _rms(x, g_ref[...]).astype(BF16)
    acc = x
    width = d_ff // n_split
    for j in range(n_split):
        gate = jnp.dot(hn, win_ref[:, j * width:(j + 1) * width], preferred_element_type=F32)
        up = jnp.dot(hn, win_ref[:, d_ff + j * width:d_ff + (j + 1) * width],
                     preferred_element_type=F32)
        a = (jax.nn.silu(gate) * up).astype(BF16)
        acc = acc + jnp.dot(a, wout_ref[j * width:(j + 1) * width, :], preferred_element_type=F32)
    if has_final:
        acc = _rms(acc, consts[3][...])
    outs[0][...] = acc


def _ffn_layer(hp, hs, g, w_in, w_out, attn=None, final=None):
    prompt_in, sample_in, consts = [hp], [hs], []
    if attn is not None:
        o_prompt, o_lat_sample, w_uvt, w_o = attn
        prompt_in.append(o_prompt)
        sample_in.append(o_lat_sample)
        consts += [w_uvt, w_o]
    consts += [g, w_in, w_out]
    if final is not None:
        consts.append(final)
    body = functools.partial(_ffn_body, d_ff=w_out.shape[0], n_split=2, has_attn=attn is not None,
                             has_final=final is not None)
    return _two_stream_call(body, prompt_in, sample_in, consts, [(D_MODEL, F32)], [(D_MODEL, F32)],
                            name="ffn")


def _kv_body(ins, consts, outs, scratch, is_sample):
    del scratch
    h_ref, tab_ref = ins
    g_ref, wdkv_ref, ln_ref, wuk_ref, wuv_ref = consts
    hn = _rms(h_ref[...], g_ref[...]).astype(BF16)
    ckr = jnp.dot(hn, wdkv_ref[...], preferred_element_type=F32)
    c = _rms(ckr[:, :KV_LORA_RANK], ln_ref[...])
    krkr = (ckr[:, KV_LORA_RANK:KV_LORA_RANK + LANES] * tab_ref[:, :LANES]
            + ckr[:, KV_LORA_RANK + LANES:] * tab_ref[:, LANES:])
    outs[0][...] = c
    outs[1][...] = krkr[:, :QK_ROPE_DIM]
    if not is_sample:
        cb = c.astype(BF16)
        outs[2][...] = jnp.dot(cb, wuk_ref[...], preferred_element_type=F32).astype(BF16)
        outs[3][...] = krkr.astype(BF16)
        outs[4][...] = jnp.dot(cb, wuv_ref[...], preferred_element_type=F32).astype(BF16)


def _kv_layer(hp, hs, tab_p, tab_s, g, w_dkv_ext, lat_norm, w_uk2, w_uv2):
    return _two_stream_call(
        _kv_body, [hp, tab_p], [hs, tab_s], [g, w_dkv_ext, lat_norm, w_uk2, w_uv2],
        [(KV_LORA_RANK, F32), (QK_ROPE_DIM, F32), (N_HEADS * QK_NOPE_DIM, BF16), (LANES, BF16),
         (N_HEADS * V_HEAD_DIM, BF16)],
        [(KV_LORA_RANK, F32), (QK_ROPE_DIM, F32), None, None, None], name="shared_kv")


def _q_body(ins, consts, outs, scratch, is_sample):
    del scratch
    h_ref, tab_ref = ins
    g_ref, wdq_ref, qn_ref, wuq_ref, wukt_ref = consts
    hn = _rms(h_ref[...], g_ref[...]).astype(BF16)
    cq = _rms(jnp.dot(hn, wdq_ref[...], preferred_element_type=F32), qn_ref[...]).astype(BF16)
    z = jnp.dot(cq, wuq_ref[...], preferred_element_type=F32) * ATTN_SCALE
    cos, sin = tab_ref[:, :LANES], tab_ref[:, LANES:]
    nope_w = N_HEADS * QK_NOPE_DIM
    rope_w = N_HEADS * QK_ROPE_DIM
    low = lax.broadcasted_iota(jnp.int32, (1, LANES), 1) < QK_ROPE_DIM
    for t in range(N_HEADS // 2):
        rt = (z[:, nope_w + t * LANES:nope_w + (t + 1) * LANES] * cos
              + z[:, nope_w + rope_w + t * LANES:nope_w + rope_w + (t + 1) * LANES] * sin)
        if is_sample:
            outs[1][:, t * LANES:(t + 1) * LANES] = rt.astype(BF16)
            continue
        for half in range(2):
            hd = 2 * t + half
            base = hd * HEAD_DIM_PAD
            outs[0][:, base:base + LANES] = z[:, hd * LANES:(hd + 1) * LANES].astype(BF16)
            keep = low if half == 0 else jnp.logical_not(low)
            outs[0][:, base + LANES:base + 2 * LANES] = jnp.where(keep, rt, 0.0).astype(BF16)
    if is_sample:
        for hd in range(N_HEADS):
            qn = z[:, hd * LANES:(hd + 1) * LANES].astype(BF16)
            outs[0][:, hd * KV_LORA_RANK:(hd + 1) * KV_LORA_RANK] = jnp.dot(
                qn, wukt_ref[hd], preferred_element_type=F32).astype(BF16)


def _q_layer(hp, hs, tab_p, tab_s, g, w_dq, q_norm, w_uq_ext, w_ukt):
    return _two_stream_call(
        _q_body, [hp, tab_p], [hs, tab_s], [g, w_dq, q_norm, w_uq_ext, w_ukt],
        [(N_HEADS * HEAD_DIM_PAD, BF16), None],
        [(N_HEADS * KV_LORA_RANK, BF16), (N_HEADS * QK_ROPE_DIM, BF16)], name="mla_query")


def _attn_kernel(pt_ref, q_ref, kn_ref, krkr_ref, v_ref, qlat_ref, qrope_ref, cnew_ref, krnew_ref,
                 cc_ref, ckr_ref, o_ref, olat_ref, cbuf, krbuf, sems, *, n_pages, page_size):
    n_seq = pl.num_programs(0) * pl.num_programs(1)
    s_idx = pl.program_id(0) * pl.num_programs(1) + pl.program_id(1)
    slot = s_idx % 2

    def start_pages(seq, sl):
        for p in range(n_pages):
            page = pt_ref[seq, p]
            keys = pl.ds(p * page_size, page_size)
            pltpu.make_async_copy(cc_ref.at[page], cbuf.at[sl, keys], sems.at[0, sl]).start()
            pltpu.make_async_copy(ckr_ref.at[page], krbuf.at[sl, :, keys], sems.at[1, sl]).start()

    def wait_pages(sl):
        pltpu.make_async_copy(cbuf.at[sl], cbuf.at[sl], sems.at[0, sl]).wait()
        pltpu.make_async_copy(krbuf.at[sl], krbuf.at[sl], sems.at[1, sl]).wait()

    @pl.when(s_idx == 0)
    def _():
        start_pages(0, 0)

    wait_pages(slot)
    start_pages(jnp.minimum(s_idx + 1, n_seq - 1), 1 - slot)

    seq = q_ref.shape[0]
    k = jnp.concatenate([kn_ref[...], krkr_ref[...]], axis=1)
    v = v_ref[...]
    row = lax.broadcasted_iota(jnp.int32, (Q_TILE, Q_TILE), 0)
    col = lax.broadcasted_iota(jnp.int32, (Q_TILE, Q_TILE), 1)
    nt = (((1,), (1,)), ((), ()))
    n_q = seq // Q_TILE
    n_chunks = n_pages * page_size // KEY_CHUNK

    def prompt_scores(qi):
        kend = (qi + 1) * Q_TILE
        q = q_ref[qi * Q_TILE:kend, :]
        s = lax.dot_general(q, k[:kend], nt, preferred_element_type=F32)
        diag = jnp.where(col <= row, s[:, kend - Q_TILE:], -jnp.inf)
        return diag if qi == 0 else jnp.concatenate([s[:, :kend - Q_TILE], diag], axis=1)

    def prompt_finish(qi, s):
        kend = (qi + 1) * Q_TILE
        p = jnp.exp(s - jnp.max(s, axis=1, keepdims=True))
        denom = jnp.sum(p, axis=1, keepdims=True)
        o = jnp.dot(p.astype(BF16), v[:kend], preferred_element_type=F32)
        o_ref[qi * Q_TILE:kend, :] = (o / denom).astype(BF16)

    q_lat = qlat_ref[0]
    q_rope = qrope_ref[0]
    c_new = cnew_ref[0]
    kr_new = krnew_ref[0]

    def sample_scores(j):
        keys = slice(j * KEY_CHUNK, (j + 1) * KEY_CHUNK)
        cb = cbuf[slot, keys, :].astype(BF16)
        krt = krbuf[slot, :, keys].astype(BF16)
        s = (lax.dot_general(q_lat, cb, nt, preferred_element_type=F32)
             + jnp.dot(q_rope, krt, preferred_element_type=F32))
        return s, cb

    def sample_update(state, s, cb):
        m, denom, acc = state
        m_new = jnp.maximum(m, jnp.max(s, axis=1, keepdims=True))
        alpha = jnp.exp(m - m_new)
        p = jnp.exp(s - m_new)
        denom = denom * alpha + jnp.sum(p, axis=1, keepdims=True)
        acc = acc * alpha + jnp.dot(p.astype(BF16), cb, preferred_element_type=F32)
        return m_new, denom, acc

    m0 = (jnp.sum(q_lat.astype(F32) * c_new, axis=1, keepdims=True)
          + jnp.sum(q_rope.astype(F32) * kr_new, axis=1, keepdims=True))
    state = (m0, jnp.ones_like(m0), jnp.broadcast_to(c_new, (N_HEADS, KV_LORA_RANK)))

    per = n_q // n_chunks
    s_next = prompt_scores(0)
    sc_next = sample_scores(0)
    for qi in range(n_q):
        s_cur = s_next
        if qi + 1 < n_q:
            s_next = prompt_scores(qi + 1)
        prompt_finish(qi, s_cur)
        if qi % per == per - 1:
            j = qi // per
            sc_cur = sc_next
            if j + 1 < n_chunks:
                sc_next = sample_scores(j + 1)
            state = sample_update(state, *sc_cur)
    _, denom, acc = state
    olat_ref[0] = (acc / denom).astype(BF16)

    @pl.when(s_idx == n_seq - 1)
    def _():
        wait_pages(1 - slot)


def _attention(page_table, q_all, kn, krkr, v, q_lat, q_rope, c_new, kr_new, cache_c, cache_krt,
               batch, seq):
    n_seq, n_pages = page_table.shape
    page_size = cache_c.shape[1]
    past = n_pages * page_size
    assert n_seq == batch * N_HEADS and past % KEY_CHUNK == 0 and seq % Q_TILE == 0
    assert (seq // Q_TILE) % (past // KEY_CHUNK) == 0

    def per_seq(width):
        return pl.BlockSpec((1,) + width, lambda b, h, pt: (b * N_HEADS + h, 0, 0))

    grid_spec = pltpu.PrefetchScalarGridSpec(
        num_scalar_prefetch=1,
        grid=(batch, N_HEADS),
        in_specs=[pl.BlockSpec((seq, HEAD_DIM_PAD), lambda b, h, pt: (b, h)),
                  pl.BlockSpec((seq, QK_NOPE_DIM), lambda b, h, pt: (b, h)),
                  pl.BlockSpec((seq, LANES), lambda b, h, pt: (b, 0)),
                  pl.BlockSpec((seq, V_HEAD_DIM), lambda b, h, pt: (b, h)),
                  per_seq((N_HEADS, KV_LORA_RANK)), per_seq((N_HEADS, QK_ROPE_DIM)),
                  per_seq((1, KV_LORA_RANK)), per_seq((1, QK_ROPE_DIM)),
                  pl.BlockSpec(memory_space=pl.ANY), pl.BlockSpec(memory_space=pl.ANY)],
        out_specs=[pl.BlockSpec((seq, V_HEAD_DIM), lambda b, h, pt: (b, h)),
                   per_seq((N_HEADS, KV_LORA_RANK))],
        scratch_shapes=[pltpu.VMEM((2, past, KV_LORA_RANK), F32),
                        pltpu.VMEM((2, QK_ROPE_DIM, past), F32),
                        pltpu.SemaphoreType.DMA((2, 2))],
    )
    return pl.pallas_call(
        functools.partial(_attn_kernel, n_pages=n_pages, page_size=page_size),
        grid_spec=grid_spec,
        out_shape=[jax.ShapeDtypeStruct((batch * seq, N_HEADS * V_HEAD_DIM), BF16),
                   jax.ShapeDtypeStruct((n_seq, N_HEADS, KV_LORA_RANK), BF16)],
        compiler_params=_params(2),
        name="attention",
    )(page_table, q_all, kn, krkr, v, q_lat, q_rope, c_new, kr_new, cache_c, cache_krt)


def _rope_table(pos):
    inv = ROPE_THETA ** (-jnp.arange(0, QK_ROPE_DIM, 2, dtype=F32) / QK_ROPE_DIM)
    ang = pos[:, None] * inv[None, :]
    cos, sin = jnp.cos(ang), jnp.sin(ang)
    reps = LANES // QK_ROPE_DIM
    return jnp.concatenate([cos, cos] * reps + [-sin, sin] * reps, axis=1)


def kernel(x_prompt, x_sample, cache_kv_latent, cache_k_rope, page_table, norm_mix, sgu_w_in,
           sgu_v_norm, sgu_w_s, sgu_b_s, sgu_w_out, norm_ffn, ffn_w_in, ffn_w_out, kv_norm, w_dkv,
           kv_latent_norm, w_uk, w_uv, w_dq, q_norm, w_uq, w_o, final_norm):
    batch, seq, _ = x_prompt.shape
    n_sample, dec_seq, _ = x_sample.shape
    n_pages = page_table.shape[1]
    page_size = cache_kv_latent.shape[1]
    past_len = n_pages * page_size
    assert dec_seq == 1 and past_len % CHUNK == 0 and seq % TOKEN_TILE == 0
    assert n_sample == CHUNK
    n_a = sgu_w_in.shape[0]
    n_b = w_dq.shape[0]
    t_prompt = batch * seq

    bf = lambda a: a.astype(BF16)
    row = lambda a: a.reshape(1, -1)
    rot = jnp.concatenate([jnp.arange(QK_ROPE_DIM // 2, QK_ROPE_DIM),
                           jnp.arange(0, QK_ROPE_DIM // 2)])
    w_kr = w_dkv[:, KV_LORA_RANK:]
    w_dkv_ext = bf(jnp.concatenate([w_dkv[:, :KV_LORA_RANK], w_kr, w_kr,
                                    w_kr[:, rot], w_kr[:, rot]], axis=1))
    w_uk2 = bf(w_uk.reshape(KV_LORA_RANK, N_HEADS * QK_NOPE_DIM))
    w_uv2 = bf(w_uv.reshape(KV_LORA_RANK, N_HEADS * V_HEAD_DIM))
    w_ukt = bf(jnp.transpose(w_uk, (1, 2, 0)))
    w_uvt = bf(jnp.transpose(w_uv, (1, 0, 2)))
    w_uq3 = w_uq.reshape(n_b, Q_LORA_RANK, N_HEADS, QK_NOPE_DIM + QK_ROPE_DIM)
    w_uq_rope = w_uq3[..., QK_NOPE_DIM:]
    w_uq_ext = bf(jnp.concatenate(
        [w_uq3[..., :QK_NOPE_DIM].reshape(n_b, Q_LORA_RANK, -1),
         w_uq_rope.reshape(n_b, Q_LORA_RANK, -1),
         w_uq_rope[..., rot].reshape(n_b, Q_LORA_RANK, -1)], axis=2))
    tab_p = _rope_table(jnp.arange(seq, dtype=F32))
    tab_s = jnp.broadcast_to(_rope_table(jnp.full((1,), past_len, F32)), (n_sample, 2 * LANES))
    cache_krt = jnp.swapaxes(cache_k_rope, 1, 2)

    hp = x_prompt.reshape(t_prompt, D_MODEL)
    hs = x_sample.reshape(n_sample, D_MODEL)
    v_rows = []
    for l in range(n_a):
        hp, hs, v_s = _sgu_layer(hp, hs, row(norm_mix[l]), bf(sgu_w_in[l]), row(sgu_v_norm[l]),
                                 sgu_w_s[l], sgu_b_s[l].T, bf(sgu_w_out[l]))
        v_rows.append(v_s)
        hp, hs = _ffn_layer(hp, hs, row(norm_ffn[l]), bf(ffn_w_in[l]), bf(ffn_w_out[l]))

    c_p, kr_p, kn, krkr, v, c_s, kr_s = _kv_layer(hp, hs, tab_p, tab_s, row(kv_norm), w_dkv_ext,
                                                  row(kv_latent_norm), w_uk2, w_uv2)
    c_new = c_s.reshape(n_sample, 1, KV_LORA_RANK)
    kr_new = kr_s.reshape(n_sample, 1, QK_ROPE_DIM)

    for j in range(n_b):
        l = n_a + j
        q_all, q_lat, q_rope = _q_layer(hp, hs, tab_p, tab_s, row(norm_mix[l]), bf(w_dq[j]),
                                        row(q_norm[j]), w_uq_ext[j], w_ukt)
        o_p, o_lat = _attention(page_table, q_all, kn, krkr, v,
                                q_lat.reshape(n_sample, N_HEADS, KV_LORA_RANK),
                                q_rope.reshape(n_sample, N_HEADS, QK_ROPE_DIM), c_new, kr_new,
                                cache_kv_latent, cache_krt, batch, seq)
        hp, hs = _ffn_layer(hp, hs, row(norm_ffn[l]), bf(ffn_w_in[l]), bf(ffn_w_out[l]),
                            attn=(o_p, o_lat.reshape(n_sample, N_HEADS * KV_LORA_RANK), w_uvt,
                                  bf(w_o[j])),
                            final=row(final_norm) if j == n_b - 1 else None)

    return (hp.reshape(batch, seq, D_MODEL),
            hs.reshape(n_sample, 1, D_MODEL),
            c_p.reshape(batch, seq, KV_LORA_RANK),
            kr_p.reshape(batch, seq, QK_ROPE_DIM),
            c_new,
            kr_new,
            jnp.stack(v_rows).reshape(n_a, n_sample, 1, D_SGU))
```

```python
import functools

import jax
import jax.numpy as jnp
from jax import lax
from jax.experimental import pallas as pl
from jax.experimental.pallas import tpu as pltpu

D_MODEL = 1024
CHUNK = 128
D_SGU = 2 * D_MODEL
SGU_GROUPS = 8
SGU_GROUP_DIM = D_SGU // SGU_GROUPS
N_HEADS = 16
QK_NOPE_DIM = 128
QK_ROPE_DIM = 64
V_HEAD_DIM = 128
Q_LORA_RANK = 384
KV_LORA_RANK = 256
ROPE_THETA = 10000.0
ATTN_SCALE = (QK_NOPE_DIM + QK_ROPE_DIM) ** -0.5
EPS = 1e-6

LANES = 128
TOKEN_TILE = 512
FFN_COLS = 2 * LANES
HEAD_DIM_PAD = 2 * LANES
Q_TILE = 256
KEY_CHUNK = 2048
VMEM_LIMIT = 56 * 1024 * 1024

F32 = jnp.float32
BF16 = jnp.bfloat16


def _rms(x, g):
    return x * lax.rsqrt(jnp.mean(x * x, axis=-1, keepdims=True) + EPS) * g


def _const_spec(const):
    if isinstance(const, tuple):
        stack, layer = const
        zeros = (0,) * (stack.ndim - 1)
        return pl.BlockSpec((None,) + stack.shape[1:], lambda *_: (layer,) + zeros,
                            pipeline_mode=pl.Buffered(1))
    zeros = (0,) * const.ndim
    return pl.BlockSpec(const.shape, lambda *_: zeros, pipeline_mode=pl.Buffered(1))


def _params(n=1):
    return pltpu.CompilerParams(dimension_semantics=("arbitrary",) * n,
                                vmem_limit_bytes=VMEM_LIMIT)


def _two_stream_call(body, prompt_in, sample_in, consts, prompt_out, sample_out, *, name,
                     scratch=(), setup=None):
    t_prompt = prompt_in[0].shape[0]
    n_sample = sample_in[0].shape[0]
    n_tiles = t_prompt // TOKEN_TILE
    assert t_prompt % TOKEN_TILE == 0
    n_in, n_c = len(prompt_in), len(consts)
    prompt_out = [po for po in prompt_out if po is not None]
    sample_out = [so for so in sample_out if so is not None]
    n_po, n_so = len(prompt_out), len(sample_out)

    def kern(*refs):
        refs = list(refs)
        p_in, refs = refs[:n_in], refs[n_in:]
        s_in, refs = refs[:n_in], refs[n_in:]
        c_in, refs = refs[:n_c], refs[n_c:]
        p_out, refs = refs[:n_po], refs[n_po:]
        s_out, scr = refs[:n_so], refs[n_so:]
        i = pl.program_id(0)
        if setup is not None:
            pl.when(i == 0)(lambda: setup(c_in, scr))
        pl.when(i < n_tiles)(lambda: body(p_in, c_in, p_out, scr, False))
        pl.when(i == n_tiles)(lambda: body(s_in, c_in, s_out, scr, True))

    def p_spec(width, rows=None):
        if rows is None:
            return pl.BlockSpec((TOKEN_TILE, width), lambda i: (jnp.minimum(i, n_tiles - 1), 0))
        period = rows // TOKEN_TILE
        return pl.BlockSpec((TOKEN_TILE, width),
                            lambda i: (jnp.minimum(i, n_tiles - 1) % period, 0))

    def s_spec(rows, width):
        return pl.BlockSpec((rows, width), lambda i: (0, 0))

    in_specs = ([p_spec(a.shape[1], None if a.shape[0] == t_prompt else a.shape[0])
                 for a in prompt_in]
                + [s_spec(*a.shape) for a in sample_in]
                + [_const_spec(c) for c in consts])
    out_specs = ([p_spec(w) for w, _ in prompt_out] + [s_spec(n_sample, w) for w, _ in sample_out])
    out_shape = ([jax.ShapeDtypeStruct((t_prompt, w), d) for w, d in prompt_out]
                 + [jax.ShapeDtypeStruct((n_sample, w), d) for w, d in sample_out])
    return pl.pallas_call(
        kern, grid=(n_tiles + 1,), in_specs=in_specs, out_specs=out_specs, out_shape=out_shape,
        scratch_shapes=list(scratch), compiler_params=_params(), name=name,
    )(*prompt_in, *sample_in, *[c[0] if isinstance(c, tuple) else c for c in consts])


def _sgu_setup(consts, scratch):
    ws_ref, bt_ref = consts[3], consts[4]
    wmix_ref, bmix_ref = scratch
    r = lax.broadcasted_iota(jnp.int32, (CHUNK, CHUNK), 0)
    c = lax.broadcasted_iota(jnp.int32, (CHUNK, CHUNK), 1)
    for g in range(SGU_GROUPS):
        w = ws_ref[g]
        wmix_ref[0, g] = jnp.where(c <= r, w, 0.0).astype(BF16)
        wmix_ref[1, g] = jnp.where(c == r, w[0:1, 0:1], 0.0).astype(BF16)
    bmix_ref[0] = bt_ref[...]
    bmix_ref[1] = jnp.broadcast_to(bt_ref[0:1, :], bt_ref.shape)


def _sgu_body(ins, consts, outs, scratch, is_sample):
    (h_ref,) = ins
    g_ref, win_ref, vn_ref, _, _, wout_ref = consts
    wmix_ref, bmix_ref = scratch
    sel = 1 if is_sample else 0
    x = h_ref[...]
    hn = _rms(x, g_ref[...]).astype(BF16)
    z = jax.nn.gelu(jnp.dot(hn, win_ref[...], preferred_element_type=F32))
    u = z[:, :D_SGU]
    v = _rms(z[:, D_SGU:], vn_ref[...])
    if is_sample:
        outs[1][...] = v
    vb = v.astype(BF16)
    bias = bmix_ref[sel]
    ys = []
    for c in range(x.shape[0] // CHUNK):
        rows = slice(c * CHUNK, (c + 1) * CHUNK)
        parts = []
        for g in range(SGU_GROUPS):
            cols = slice(g * SGU_GROUP_DIM, (g + 1) * SGU_GROUP_DIM)
            m = jnp.dot(wmix_ref[sel, g], vb[rows, cols], preferred_element_type=F32)
            parts.append(m + bias[:, g:g + 1])
        ys.append((u[rows] * jnp.concatenate(parts, axis=1)).astype(BF16))
    y = ys[0] if len(ys) == 1 else jnp.concatenate(ys, axis=0)
    outs[0][...] = x + jnp.dot(y, wout_ref[...], preferred_element_type=F32)


def _sgu_layer(hp, hs, g, w_in, v_norm, w_s, b_t, w_out):
    return _two_stream_call(
        _sgu_body, [hp], [hs], [g, w_in, v_norm, w_s, b_t, w_out],
        [(D_MODEL, F32), None], [(D_MODEL, F32), (D_SGU, F32)], name="sgu_mixer", setup=_sgu_setup,
        scratch=[pltpu.VMEM((2, SGU_GROUPS, CHUNK, CHUNK), BF16),
                 pltpu.VMEM((2, CHUNK, SGU_GROUPS), F32)])


def _ffn_body(ins, consts, outs, scratch, is_sample, *, d_ff, n_split, has_attn, has_final):
    del scratch
    consts = list(consts)
    x = ins[0][...]
    if has_attn:
        wuv_ref, wo_ref = consts.pop(0), consts.pop(0)
        if is_sample:
            o = jnp.concatenate(
                [jnp.dot(ins[1][:, hd * KV_LORA_RANK:(hd + 1) * KV_LORA_RANK], wuv_ref[hd],
                         preferred_element_type=F32).astype(BF16) for hd in range(N_HEADS)], axis=1)
        else:
            o = ins[1][...]
        x = x + jnp.dot(o, wo_ref[...], preferred_element_type=F32)
    g_ref, win_ref, wout_ref = consts[:3]
    hn = _rms(x, g_ref[...]).astype(BF16)
    width = d_ff // n_split

    def gate_up(j):
        return (jnp.dot(hn, win_ref[:, j * width:(j + 1) * width], preferred_element_type=F32),
                jnp.dot(hn, win_ref[:, d_ff + j * width:d_ff + (j + 1) * width],
                        preferred_element_type=F32))

    acts = []
    nxt = gate_up(0)
    for j in range(n_split):
        gate, up = nxt
        if j + 1 < n_split:
            nxt = gate_up(j + 1)
        acts.append((jax.nn.silu(gate) * up).astype(BF16))
    acc = x + jnp.dot(jnp.concatenate(acts, axis=1), wout_ref[...], preferred_element_type=F32)
    if has_final:
        acc = _rms(acc, consts[3][...])
    outs[0][...] = acc


def _ffn_layer(hp, hs, g, w_in, w_out, attn=None, final=None):
    prompt_in, sample_in, consts = [hp], [hs], []
    if attn is not None:
        o_prompt, o_lat_sample, w_uvt, w_o = attn
        prompt_in.append(o_prompt)
        sample_in.append(o_lat_sample)
        consts += [w_uvt, w_o]
    consts += [g, w_in, w_out]
    if final is not None:
        consts.append(final)
    d_ff = w_out[0].shape[-2]
    assert d_ff % FFN_COLS == 0
    body = functools.partial(_ffn_body, d_ff=d_ff, n_split=d_ff // FFN_COLS,
                             has_attn=attn is not None, has_final=final is not None)
    return _two_stream_call(body, prompt_in, sample_in, consts, [(D_MODEL, F32)], [(D_MODEL, F32)],
                            name="ffn")


def _kv_body(ins, consts, outs, scratch, is_sample):
    del scratch
    h_ref, tab_ref = ins
    g_ref, wdkv_ref, ln_ref, wuk_ref, wuv_ref = consts
    hn = _rms(h_ref[...], g_ref[...]).astype(BF16)
    ckr = jnp.dot(hn, wdkv_ref[...], preferred_element_type=F32)
    c = _rms(ckr[:, :KV_LORA_RANK], ln_ref[...])
    krkr = (ckr[:, KV_LORA_RANK:KV_LORA_RANK + LANES] * tab_ref[:, :LANES]
            + ckr[:, KV_LORA_RANK + LANES:] * tab_ref[:, LANES:])
    outs[0][...] = c
    outs[1][...] = krkr[:, :QK_ROPE_DIM]
    if not is_sample:
        cb = c.astype(BF16)
        outs[2][...] = jnp.dot(cb, wuk_ref[...], preferred_element_type=F32).astype(BF16)
        outs[3][...] = krkr.astype(BF16)
        outs[4][...] = jnp.dot(cb, wuv_ref[...], preferred_element_type=F32).astype(BF16)


def _kv_layer(hp, hs, tab_p, tab_s, g, w_dkv_ext, lat_norm, w_uk2, w_uv2):
    return _two_stream_call(
        _kv_body, [hp, tab_p], [hs, tab_s], [g, w_dkv_ext, lat_norm, w_uk2, w_uv2],
        [(KV_LORA_RANK, F32), (QK_ROPE_DIM, F32), (N_HEADS * QK_NOPE_DIM, BF16), (LANES, BF16),
         (N_HEADS * V_HEAD_DIM, BF16)],
        [(KV_LORA_RANK, F32), (QK_ROPE_DIM, F32), None, None, None], name="shared_kv")


def _q_body(ins, consts, outs, scratch, is_sample):
    del scratch
    h_ref, tab_ref = ins
    g_ref, wdq_ref, qn_ref, wuq_ref, wukt_ref = consts
    hn = _rms(h_ref[...], g_ref[...]).astype(BF16)
    cq = _rms(jnp.dot(hn, wdq_ref[...], preferred_element_type=F32), qn_ref[...]).astype(BF16)
    z = jnp.dot(cq, wuq_ref[...], preferred_element_type=F32) * ATTN_SCALE
    cos, sin = tab_ref[:, :LANES], tab_ref[:, LANES:]
    nope_w = N_HEADS * QK_NOPE_DIM
    rope_w = N_HEADS * QK_ROPE_DIM
    low = lax.broadcasted_iota(jnp.int32, (1, LANES), 1) < QK_ROPE_DIM
    for t in range(N_HEADS // 2):
        rt = (z[:, nope_w + t * LANES:nope_w + (t + 1) * LANES] * cos
              + z[:, nope_w + rope_w + t * LANES:nope_w + rope_w + (t + 1) * LANES] * sin)
        if is_sample:
            outs[1][:, t * LANES:(t + 1) * LANES] = rt.astype(BF16)
            continue
        for half in range(2):
            hd = 2 * t + half
            base = hd * HEAD_DIM_PAD
            outs[0][:, base:base + LANES] = z[:, hd * LANES:(hd + 1) * LANES].astype(BF16)
            keep = low if half == 0 else jnp.logical_not(low)
            outs[0][:, base + LANES:base + 2 * LANES] = jnp.where(keep, rt, 0.0).astype(BF16)
    if is_sample:
        for hd in range(N_HEADS):
            qn = z[:, hd * LANES:(hd + 1) * LANES].astype(BF16)
            outs[0][:, hd * KV_LORA_RANK:(hd + 1) * KV_LORA_RANK] = jnp.dot(
                qn, wukt_ref[hd], preferred_element_type=F32).astype(BF16)


def _q_layer(hp, hs, tab_p, tab_s, g, w_dq, q_norm, w_uq_ext, w_ukt):
    return _two_stream_call(
        _q_body, [hp, tab_p], [hs, tab_s], [g, w_dq, q_norm, w_uq_ext, w_ukt],
        [(N_HEADS * HEAD_DIM_PAD, BF16), None],
        [(N_HEADS * KV_LORA_RANK, BF16), (N_HEADS * QK_ROPE_DIM, BF16)], name="mla_query")


def _attn_kernel(pt_ref, q_ref, kn_ref, krkr_ref, v_ref, qlat_ref, qrope_ref, cnew_ref, krnew_ref,
                 cc_ref, ckr_ref, o_ref, olat_ref, cbuf, krbuf, sems, *, n_pages, page_size):
    n_seq = pl.num_programs(0) * pl.num_programs(1)
    s_idx = pl.program_id(0) * pl.num_programs(1) + pl.program_id(1)
    slot = s_idx % 2

    def start_pages(seq, sl, pages):
        for p in pages:
            page = pt_ref[seq, p]
            keys = pl.ds(p * page_size, page_size)
            pltpu.make_async_copy(cc_ref.at[page], cbuf.at[sl, keys], sems.at[0, sl]).start()
            pltpu.make_async_copy(ckr_ref.at[page], krbuf.at[sl, :, keys], sems.at[1, sl]).start()

    def wait_pages(sl):
        pltpu.make_async_copy(cbuf.at[sl], cbuf.at[sl], sems.at[0, sl]).wait()
        pltpu.make_async_copy(krbuf.at[sl], krbuf.at[sl], sems.at[1, sl]).wait()

    @pl.when(s_idx == 0)
    def _():
        start_pages(0, 0, range(n_pages))

    wait_pages(slot)
    next_seq = jnp.minimum(s_idx + 1, n_seq - 1)

    seq = q_ref.shape[0]
    k = jnp.concatenate([kn_ref[...], krkr_ref[...]], axis=1)
    v = v_ref[...]
    row = lax.broadcasted_iota(jnp.int32, (Q_TILE, Q_TILE), 0)
    col = lax.broadcasted_iota(jnp.int32, (Q_TILE, Q_TILE), 1)
    nt = (((1,), (1,)), ((), ()))
    n_q = seq // Q_TILE
    n_chunks = n_pages * page_size // KEY_CHUNK

    def prompt_scores(qi):
        kend = (qi + 1) * Q_TILE
        q = q_ref[qi * Q_TILE:kend, :]
        s = lax.dot_general(q, k[:kend], nt, preferred_element_type=F32)
        diag = jnp.where(col <= row, s[:, kend - Q_TILE:], -jnp.inf)
        return diag if qi == 0 else jnp.concatenate([s[:, :kend - Q_TILE], diag], axis=1)

    def prompt_finish(qi, s):
        kend = (qi + 1) * Q_TILE
        p = jnp.exp(s - jnp.max(s, axis=1, keepdims=True))
        denom = jnp.sum(p, axis=1, keepdims=True)
        o = jnp.dot(p.astype(BF16), v[:kend], preferred_element_type=F32)
        o_ref[qi * Q_TILE:kend, :] = (o / denom).astype(BF16)

    q_lat = qlat_ref[0]
    q_rope = qrope_ref[0]
    c_new = cnew_ref[0]
    kr_new = krnew_ref[0]

    def sample_scores(j):
        keys = slice(j * KEY_CHUNK, (j + 1) * KEY_CHUNK)
        cb = cbuf[slot, keys, :].astype(BF16)
        krt = krbuf[slot, :, keys].astype(BF16)
        s = (lax.dot_general(q_lat, cb, nt, preferred_element_type=F32)
             + jnp.dot(q_rope, krt, preferred_element_type=F32))
        return s, cb

    def sample_update(state, s, cb):
        m, denom, acc = state
        m_new = jnp.maximum(m, jnp.max(s, axis=1, keepdims=True))
        alpha = jnp.exp(m - m_new)
        p = jnp.exp(s - m_new)
        denom = denom * alpha + jnp.sum(p, axis=1, keepdims=True)
        acc = acc * alpha + jnp.dot(p.astype(BF16), cb, preferred_element_type=F32)
        return m_new, denom, acc

    m0 = (jnp.sum(q_lat.astype(F32) * c_new, axis=1, keepdims=True)
          + jnp.sum(q_rope.astype(F32) * kr_new, axis=1, keepdims=True))
    state = (m0, jnp.ones_like(m0), jnp.broadcast_to(c_new, (N_HEADS, KV_LORA_RANK)))

    per = n_q // n_chunks
    pages_per_block = n_pages // n_q
    s_next = prompt_scores(0)
    sc_next = sample_scores(0)
    for qi in range(n_q):
        s_cur = s_next
        if qi + 1 < n_q:
            s_next = prompt_scores(qi + 1)
        prompt_finish(qi, s_cur)
        start_pages(next_seq, 1 - slot, range(qi * pages_per_block, (qi + 1) * pages_per_block))
        if qi % per == per - 1:
            j = qi // per
            sc_cur = sc_next
            if j + 1 < n_chunks:
                sc_next = sample_scores(j + 1)
            state = sample_update(state, *sc_cur)
    _, denom, acc = state
    olat_ref[0] = (acc / denom).astype(BF16)

    @pl.when(s_idx == n_seq - 1)
    def _():
        wait_pages(1 - slot)


def _attention(page_table, q_all, kn, krkr, v, q_lat, q_rope, c_new, kr_new, cache_c, cache_krt,
               batch, seq):
    n_seq, n_pages = page_table.shape
    page_size = cache_c.shape[1]
    past = n_pages * page_size
    assert n_seq == batch * N_HEADS and past % KEY_CHUNK == 0 and seq % Q_TILE == 0
    assert (seq // Q_TILE) % (past // KEY_CHUNK) == 0 and n_pages % (seq // Q_TILE) == 0

    def per_seq(width):
        return pl.BlockSpec((1,) + width, lambda b, h, pt: (b * N_HEADS + h, 0, 0))

    grid_spec = pltpu.PrefetchScalarGridSpec(
        num_scalar_prefetch=1,
        grid=(batch, N_HEADS),
        in_specs=[pl.BlockSpec((seq, HEAD_DIM_PAD), lambda b, h, pt: (b, h)),
                  pl.BlockSpec((seq, QK_NOPE_DIM), lambda b, h, pt: (b, h)),
                  pl.BlockSpec((seq, LANES), lambda b, h, pt: (b, 0)),
                  pl.BlockSpec((seq, V_HEAD_DIM), lambda b, h, pt: (b, h)),
                  per_seq((N_HEADS, KV_LORA_RANK)), per_seq((N_HEADS, QK_ROPE_DIM)),
                  per_seq((1, KV_LORA_RANK)), per_seq((1, QK_ROPE_DIM)),
                  pl.BlockSpec(memory_space=pl.ANY), pl.BlockSpec(memory_space=pl.ANY)],
        out_specs=[pl.BlockSpec((seq, V_HEAD_DIM), lambda b, h, pt: (b, h)),
                   per_seq((N_HEADS, KV_LORA_RANK))],
        scratch_shapes=[pltpu.VMEM((2, past, KV_LORA_RANK), F32),
                        pltpu.VMEM((2, QK_ROPE_DIM, past), F32),
                        pltpu.SemaphoreType.DMA((2, 2))],
    )
    return pl.pallas_call(
        functools.partial(_attn_kernel, n_pages=n_pages, page_size=page_size),
        grid_spec=grid_spec,
        out_shape=[jax.ShapeDtypeStruct((batch * seq, N_HEADS * V_HEAD_DIM), BF16),
                   jax.ShapeDtypeStruct((n_seq, N_HEADS, KV_LORA_RANK), BF16)],
        compiler_params=_params(2),
        name="attention",
    )(page_table, q_all, kn, krkr, v, q_lat, q_rope, c_new, kr_new, cache_c, cache_krt)


def _rope_table(pos):
    inv = ROPE_THETA ** (-jnp.arange(0, QK_ROPE_DIM, 2, dtype=F32) / QK_ROPE_DIM)
    ang = pos[:, None] * inv[None, :]
    cos, sin = jnp.cos(ang), jnp.sin(ang)
    reps = LANES // QK_ROPE_DIM
    return jnp.concatenate([cos, cos] * reps + [-sin, sin] * reps, axis=1)


def kernel(x_prompt, x_sample, cache_kv_latent, cache_k_rope, page_table, norm_mix, sgu_w_in,
           sgu_v_norm, sgu_w_s, sgu_b_s, sgu_w_out, norm_ffn, ffn_w_in, ffn_w_out, kv_norm, w_dkv,
           kv_latent_norm, w_uk, w_uv, w_dq, q_norm, w_uq, w_o, final_norm):
    batch, seq, _ = x_prompt.shape
    n_sample, dec_seq, _ = x_sample.shape
    n_pages = page_table.shape[1]
    page_size = cache_kv_latent.shape[1]
    past_len = n_pages * page_size
    assert dec_seq == 1 and past_len % CHUNK == 0 and seq % TOKEN_TILE == 0
    assert n_sample == CHUNK
    n_a = sgu_w_in.shape[0]
    n_b = w_dq.shape[0]
    t_prompt = batch * seq

    bf = lambda a: a.astype(BF16)
    row = lambda a: a.reshape(1, -1)
    rot = jnp.concatenate([jnp.arange(QK_ROPE_DIM // 2, QK_ROPE_DIM),
                           jnp.arange(0, QK_ROPE_DIM // 2)])
    w_kr = w_dkv[:, KV_LORA_RANK:]
    w_dkv_ext = bf(jnp.concatenate([w_dkv[:, :KV_LORA_RANK], w_kr, w_kr,
                                    w_kr[:, rot], w_kr[:, rot]], axis=1))
    w_uk2 = bf(w_uk.reshape(KV_LORA_RANK, N_HEADS * QK_NOPE_DIM))
    w_uv2 = bf(w_uv.reshape(KV_LORA_RANK, N_HEADS * V_HEAD_DIM))
    w_ukt = bf(jnp.transpose(w_uk, (1, 2, 0)))
    w_uvt = bf(jnp.transpose(w_uv, (1, 0, 2)))
    w_uq3 = w_uq.reshape(n_b, Q_LORA_RANK, N_HEADS, QK_NOPE_DIM + QK_ROPE_DIM)
    w_uq_rope = w_uq3[..., QK_NOPE_DIM:]
    w_uq_ext = bf(jnp.concatenate(
        [w_uq3[..., :QK_NOPE_DIM].reshape(n_b, Q_LORA_RANK, -1),
         w_uq_rope.reshape(n_b, Q_LORA_RANK, -1),
         w_uq_rope[..., rot].reshape(n_b, Q_LORA_RANK, -1)], axis=2))
    tab_p = _rope_table(jnp.arange(seq, dtype=F32))
    tab_s = jnp.broadcast_to(_rope_table(jnp.full((1,), past_len, F32)), (n_sample, 2 * LANES))
    cache_krt = jnp.swapaxes(cache_k_rope, 1, 2)

    rows = lambda a: a.reshape(a.shape[0], 1, a.shape[1])
    norm_mix, norm_ffn, sgu_v_norm, q_norm = map(rows, (norm_mix, norm_ffn, sgu_v_norm, q_norm))
    sgu_w_in, sgu_w_out, ffn_w_in, ffn_w_out, w_dq, w_o = map(
        bf, (sgu_w_in, sgu_w_out, ffn_w_in, ffn_w_out, w_dq, w_o))
    sgu_b_t = jnp.swapaxes(sgu_b_s, 1, 2)

    hp = x_prompt.reshape(t_prompt, D_MODEL)
    hs = x_sample.reshape(n_sample, D_MODEL)
    v_rows = []
    for l in range(n_a):
        hp, hs, v_s = _sgu_layer(hp, hs, (norm_mix, l), (sgu_w_in, l), (sgu_v_norm, l),
                                 (sgu_w_s, l), (sgu_b_t, l), (sgu_w_out, l))
        v_rows.append(v_s)
        hp, hs = _ffn_layer(hp, hs, (norm_ffn, l), (ffn_w_in, l), (ffn_w_out, l))

    c_p, kr_p, kn, krkr, v, c_s, kr_s = _kv_layer(hp, hs, tab_p, tab_s, row(kv_norm), w_dkv_ext,
                                                  row(kv_latent_norm), w_uk2, w_uv2)
    c_new = c_s.reshape(n_sample, 1, KV_LORA_RANK)
    kr_new = kr_s.reshape(n_sample, 1, QK_ROPE_DIM)

    for j in range(n_b):
        l = n_a + j
        q_all, q_lat, q_rope = _q_layer(hp, hs, tab_p, tab_s, (norm_mix, l), (w_dq, j),
                                        (q_norm, j), (w_uq_ext, j), w_ukt)
        o_p, o_lat = _attention(page_table, q_all, kn, krkr, v,
                                q_lat.reshape(n_sample, N_HEADS, KV_LORA_RANK),
                                q_rope.reshape(n_sample, N_HEADS, QK_ROPE_DIM), c_new, kr_new,
                                cache_kv_latent, cache_krt, batch, seq)
        hp, hs = _ffn_layer(hp, hs, (norm_ffn, l), (ffn_w_in, l), (ffn_w_out, l),
                            attn=(o_p, o_lat.reshape(n_sample, N_HEADS * KV_LORA_RANK), w_uvt,
                                  (w_o, j)),
                            final=row(final_norm) if j == n_b - 1 else None)

    return (hp.reshape(batch, seq, D_MODEL),
            hs.reshape(n_sample, 1, D_MODEL),
            c_p.reshape(batch, seq, KV_LORA_RANK),
            kr_p.reshape(batch, seq, QK_ROPE_DIM),
            c_new,
            kr_new,
            jnp.stack(v_rows).reshape(n_a, n_sample, 1, D_SGU))
```

```python
import functools

import jax
import jax.numpy as jnp
from jax import lax
from jax.experimental import pallas as pl
from jax.experimental.pallas import tpu as pltpu

D_MODEL = 1024
CHUNK = 128
D_SGU = 2 * D_MODEL
SGU_GROUPS = 8
SGU_GROUP_DIM = D_SGU // SGU_GROUPS
N_HEADS = 16
QK_NOPE_DIM = 128
QK_ROPE_DIM = 64
V_HEAD_DIM = 128
Q_LORA_RANK = 384
KV_LORA_RANK = 256
ROPE_THETA = 10000.0
ATTN_SCALE = (QK_NOPE_DIM + QK_ROPE_DIM) ** -0.5
EPS = 1e-6

LANES = 128
TOKEN_TILE = 512
FFN_COLS = 2 * LANES
HEAD_DIM_PAD = 2 * LANES
Q_TILE = 256
KEY_CHUNK = 2048
VMEM_LIMIT = 56 * 1024 * 1024

F32 = jnp.float32
BF16 = jnp.bfloat16


def _rms(x, g):
    return x * lax.rsqrt(jnp.mean(x * x, axis=-1, keepdims=True) + EPS) * g


def _const_spec(const):
    if isinstance(const, tuple):
        stack, layer = const
        zeros = (0,) * (stack.ndim - 1)
        return pl.BlockSpec((None,) + stack.shape[1:], lambda *_: (layer,) + zeros,
                            pipeline_mode=pl.Buffered(1))
    zeros = (0,) * const.ndim
    return pl.BlockSpec(const.shape, lambda *_: zeros, pipeline_mode=pl.Buffered(1))


def _params(n=1):
    return pltpu.CompilerParams(dimension_semantics=("arbitrary",) * n,
                                vmem_limit_bytes=VMEM_LIMIT)


def _two_stream_call(body, prompt_in, sample_in, consts, prompt_out, sample_out, *, name,
                     scratch=(), setup=None):
    t_prompt = prompt_in[0].shape[0]
    n_sample = sample_in[0].shape[0]
    n_tiles = t_prompt // TOKEN_TILE
    assert t_prompt % TOKEN_TILE == 0
    n_in, n_c = len(prompt_in), len(consts)
    prompt_out = [po for po in prompt_out if po is not None]
    sample_out = [so for so in sample_out if so is not None]
    n_po, n_so = len(prompt_out), len(sample_out)

    def kern(*refs):
        refs = list(refs)
        p_in, refs = refs[:n_in], refs[n_in:]
        s_in, refs = refs[:n_in], refs[n_in:]
        c_in, refs = refs[:n_c], refs[n_c:]
        p_out, refs = refs[:n_po], refs[n_po:]
        s_out, scr = refs[:n_so], refs[n_so:]
        i = pl.program_id(0)
        if setup is not None:
            pl.when(i == 0)(lambda: setup(c_in, scr))
        pl.when(i < n_tiles)(lambda: body(p_in, c_in, p_out, scr, False))
        pl.when(i == n_tiles)(lambda: body(s_in, c_in, s_out, scr, True))

    def p_spec(width, rows=None):
        if rows is None:
            return pl.BlockSpec((TOKEN_TILE, width), lambda i: (jnp.minimum(i, n_tiles - 1), 0))
        period = rows // TOKEN_TILE
        return pl.BlockSpec((TOKEN_TILE, width),
                            lambda i: (jnp.minimum(i, n_tiles - 1) % period, 0))

    def s_spec(rows, width):
        return pl.BlockSpec((rows, width), lambda i: (0, 0))

    in_specs = ([p_spec(a.shape[1], None if a.shape[0] == t_prompt else a.shape[0])
                 for a in prompt_in]
                + [s_spec(*a.shape) for a in sample_in]
                + [_const_spec(c) for c in consts])
    out_specs = ([p_spec(w) for w, _ in prompt_out] + [s_spec(n_sample, w) for w, _ in sample_out])
    out_shape = ([jax.ShapeDtypeStruct((t_prompt, w), d) for w, d in prompt_out]
                 + [jax.ShapeDtypeStruct((n_sample, w), d) for w, d in sample_out])
    return pl.pallas_call(
        kern, grid=(n_tiles + 1,), in_specs=in_specs, out_specs=out_specs, out_shape=out_shape,
        scratch_shapes=list(scratch), compiler_params=_params(), name=name,
    )(*prompt_in, *sample_in, *[c[0] if isinstance(c, tuple) else c for c in consts])


def _sgu_setup(consts, scratch):
    ws_ref, bt_ref = consts[3], consts[4]
    wmix_ref, bmix_ref = scratch
    r = lax.broadcasted_iota(jnp.int32, (CHUNK, CHUNK), 0)
    c = lax.broadcasted_iota(jnp.int32, (CHUNK, CHUNK), 1)
    for g in range(SGU_GROUPS):
        w = ws_ref[g]
        wmix_ref[0, g] = jnp.where(c <= r, w, 0.0).astype(BF16)
        wmix_ref[1, g] = jnp.where(c == r, w[0:1, 0:1], 0.0).astype(BF16)
    bmix_ref[0] = bt_ref[...]
    bmix_ref[1] = jnp.broadcast_to(bt_ref[0:1, :], bt_ref.shape)


def _sgu_body(ins, consts, outs, scratch, is_sample):
    (h_ref,) = ins
    g_ref, win_ref, vn_ref, _, _, wout_ref = consts
    wmix_ref, bmix_ref = scratch
    sel = 1 if is_sample else 0
    x = h_ref[...]
    hn = _rms(x, g_ref[...]).astype(BF16)
    z = jax.nn.gelu(jnp.dot(hn, win_ref[...], preferred_element_type=F32))
    u = z[:, :D_SGU]
    v = _rms(z[:, D_SGU:], vn_ref[...])
    if is_sample:
        outs[1][...] = v
    vb = v.astype(BF16)
    bias = bmix_ref[sel]
    ys = []
    for c in range(x.shape[0] // CHUNK):
        rows = slice(c * CHUNK, (c + 1) * CHUNK)
        parts = []
        for g in range(SGU_GROUPS):
            cols = slice(g * SGU_GROUP_DIM, (g + 1) * SGU_GROUP_DIM)
            m = jnp.dot(wmix_ref[sel, g], vb[rows, cols], preferred_element_type=F32)
            parts.append(m + bias[:, g:g + 1])
        ys.append((u[rows] * jnp.concatenate(parts, axis=1)).astype(BF16))
    y = ys[0] if len(ys) == 1 else jnp.concatenate(ys, axis=0)
    outs[0][...] = x + jnp.dot(y, wout_ref[...], preferred_element_type=F32)


def _sgu_layer(hp, hs, g, w_in, v_norm, w_s, b_t, w_out):
    return _two_stream_call(
        _sgu_body, [hp], [hs], [g, w_in, v_norm, w_s, b_t, w_out],
        [(D_MODEL, F32), None], [(D_MODEL, F32), (D_SGU, F32)], name="sgu_mixer", setup=_sgu_setup,
        scratch=[pltpu.VMEM((2, SGU_GROUPS, CHUNK, CHUNK), BF16),
                 pltpu.VMEM((2, CHUNK, SGU_GROUPS), F32)])


def _ffn_body(ins, consts, outs, scratch, is_sample, *, d_ff, n_split, has_attn, has_final):
    del scratch
    consts = list(consts)
    x = ins[0][...]
    if has_attn:
        wuv_ref, wo_ref = consts.pop(0), consts.pop(0)
        if is_sample:
            o = jnp.concatenate(
                [jnp.dot(ins[1][:, hd * KV_LORA_RANK:(hd + 1) * KV_LORA_RANK], wuv_ref[hd],
                         preferred_element_type=F32).astype(BF16) for hd in range(N_HEADS)], axis=1)
        else:
            o = ins[1][...]
        x = x + jnp.dot(o, wo_ref[...], preferred_element_type=F32)
    g_ref, win_ref, wout_ref = consts[:3]
    hn = _rms(x, g_ref[...]).astype(BF16)
    width = d_ff // n_split

    def gate_up(j):
        return (jnp.dot(hn, win_ref[:, j * width:(j + 1) * width], preferred_element_type=F32),
                jnp.dot(hn, win_ref[:, d_ff + j * width:d_ff + (j + 1) * width],
                        preferred_element_type=F32))

    acts = []
    nxt = gate_up(0)
    for j in range(n_split):
        gate, up = nxt
        if j + 1 < n_split:
            nxt = gate_up(j + 1)
        acts.append((jax.nn.silu(gate) * up).astype(BF16))
    acc = x + jnp.dot(jnp.concatenate(acts, axis=1), wout_ref[...], preferred_element_type=F32)
    if has_final:
        acc = _rms(acc, consts[3][...])
    outs[0][...] = acc


def _ffn_layer(hp, hs, g, w_in, w_out, attn=None, final=None):
    prompt_in, sample_in, consts = [hp], [hs], []
    if attn is not None:
        o_prompt, o_lat_sample, w_uvt, w_o = attn
        prompt_in.append(o_prompt)
        sample_in.append(o_lat_sample)
        consts += [w_uvt, w_o]
    consts += [g, w_in, w_out]
    if final is not None:
        consts.append(final)
    d_ff = w_out[0].shape[-2]
    assert d_ff % FFN_COLS == 0
    body = functools.partial(_ffn_body, d_ff=d_ff, n_split=d_ff // FFN_COLS,
                             has_attn=attn is not None, has_final=final is not None)
    return _two_stream_call(body, prompt_in, sample_in, consts, [(D_MODEL, F32)], [(D_MODEL, F32)],
                            name="ffn")


def _kv_body(ins, consts, outs, scratch, is_sample):
    del scratch
    h_ref, tab_ref = ins
    g_ref, wdkv_ref, ln_ref, wuk_ref, wuv_ref = consts
    hn = _rms(h_ref[...], g_ref[...]).astype(BF16)
    ckr = jnp.dot(hn, wdkv_ref[...], preferred_element_type=F32)
    c = _rms(ckr[:, :KV_LORA_RANK], ln_ref[...])
    krkr = (ckr[:, KV_LORA_RANK:KV_LORA_RANK + LANES] * tab_ref[:, :LANES]
            + ckr[:, KV_LORA_RANK + LANES:] * tab_ref[:, LANES:])
    outs[0][...] = c
    outs[1][...] = krkr[:, :QK_ROPE_DIM]
    if not is_sample:
        cb = c.astype(BF16)
        outs[2][...] = jnp.dot(cb, wuk_ref[...], preferred_element_type=F32).astype(BF16)
        outs[3][...] = krkr.astype(BF16)
        outs[4][...] = jnp.dot(cb, wuv_ref[...], preferred_element_type=F32).astype(BF16)


def _kv_layer(hp, hs, tab_p, tab_s, g, w_dkv_ext, lat_norm, w_uk2, w_uv2):
    return _two_stream_call(
        _kv_body, [hp, tab_p], [hs, tab_s], [g, w_dkv_ext, lat_norm, w_uk2, w_uv2],
        [(KV_LORA_RANK, F32), (QK_ROPE_DIM, F32), (N_HEADS * QK_NOPE_DIM, BF16), (LANES, BF16),
         (N_HEADS * V_HEAD_DIM, BF16)],
        [(KV_LORA_RANK, F32), (QK_ROPE_DIM, F32), None, None, None], name="shared_kv")


def _q_body(ins, consts, outs, scratch, is_sample):
    del scratch
    h_ref, tab_ref = ins
    g_ref, wdq_ref, qn_ref, wuq_ref, wukt_ref = consts
    hn = _rms(h_ref[...], g_ref[...]).astype(BF16)
    cq = _rms(jnp.dot(hn, wdq_ref[...], preferred_element_type=F32), qn_ref[...]).astype(BF16)
    z = jnp.dot(cq, wuq_ref[...], preferred_element_type=F32) * ATTN_SCALE
    cos, sin = tab_ref[:, :LANES], tab_ref[:, LANES:]
    nope_w = N_HEADS * QK_NOPE_DIM
    rope_w = N_HEADS * QK_ROPE_DIM
    low = lax.broadcasted_iota(jnp.int32, (1, LANES), 1) < QK_ROPE_DIM
    for t in range(N_HEADS // 2):
        rt = (z[:, nope_w + t * LANES:nope_w + (t + 1) * LANES] * cos
              + z[:, nope_w + rope_w + t * LANES:nope_w + rope_w + (t + 1) * LANES] * sin)
        if is_sample:
            outs[1][:, t * LANES:(t + 1) * LANES] = rt.astype(BF16)
            continue
        for half in range(2):
            hd = 2 * t + half
            base = hd * HEAD_DIM_PAD
            outs[0][:, base:base + LANES] = z[:, hd * LANES:(hd + 1) * LANES].astype(BF16)
            keep = low if half == 0 else jnp.logical_not(low)
            outs[0][:, base + LANES:base + 2 * LANES] = jnp.where(keep, rt, 0.0).astype(BF16)
    if is_sample:
        for hd in range(N_HEADS):
            qn = z[:, hd * LANES:(hd + 1) * LANES].astype(BF16)
            outs[0][:, hd * KV_LORA_RANK:(hd + 1) * KV_LORA_RANK] = jnp.dot(
                qn, wukt_ref[hd], preferred_element_type=F32).astype(BF16)


def _q_layer(hp, hs, tab_p, tab_s, g, w_dq, q_norm, w_uq_ext, w_ukt):
    return _two_stream_call(
        _q_body, [hp, tab_p], [hs, tab_s], [g, w_dq, q_norm, w_uq_ext, w_ukt],
        [(N_HEADS * HEAD_DIM_PAD, BF16), None],
        [(N_HEADS * KV_LORA_RANK, BF16), (N_HEADS * QK_ROPE_DIM, BF16)], name="mla_query")


def _attn_kernel(pt_ref, q_ref, kn_ref, krkr_ref, v_ref, qlat_ref, qrope_ref, cnew_ref, krnew_ref,
                 cc_ref, ckr_ref, o_ref, olat_ref, cbuf, krbuf, sems, *, n_pages, page_size):
    n_seq = pl.num_programs(0) * pl.num_programs(1)
    s_idx = pl.program_id(0) * pl.num_programs(1) + pl.program_id(1)
    slot = s_idx % 2

    def start_pages(seq, sl, pages):
        for p in pages:
            page = pt_ref[seq, p]
            keys = pl.ds(p * page_size, page_size)
            pltpu.make_async_copy(cc_ref.at[page], cbuf.at[sl, keys], sems.at[0, sl]).start()
            pltpu.make_async_copy(ckr_ref.at[page], krbuf.at[sl, :, keys], sems.at[1, sl]).start()

    def wait_pages(sl):
        pltpu.make_async_copy(cbuf.at[sl], cbuf.at[sl], sems.at[0, sl]).wait()
        pltpu.make_async_copy(krbuf.at[sl], krbuf.at[sl], sems.at[1, sl]).wait()

    @pl.when(s_idx == 0)
    def _():
        start_pages(0, 0, range(n_pages))

    wait_pages(slot)
    start_pages(jnp.minimum(s_idx + 1, n_seq - 1), 1 - slot, range(n_pages))

    seq = q_ref.shape[0]
    k = jnp.concatenate([kn_ref[...], krkr_ref[...]], axis=1)
    vt = v_ref[...].T
    row = lax.broadcasted_iota(jnp.int32, (Q_TILE, Q_TILE), 0)
    col = lax.broadcasted_iota(jnp.int32, (Q_TILE, Q_TILE), 1)
    nt = (((1,), (1,)), ((), ()))
    n_q = seq // Q_TILE
    n_chunks = n_pages * page_size // KEY_CHUNK

    def prompt_scores(qi):
        kend = (qi + 1) * Q_TILE
        q = q_ref[qi * Q_TILE:kend, :]
        s = lax.dot_general(k[:kend], q, nt, preferred_element_type=F32)
        diag = jnp.where(row <= col, s[kend - Q_TILE:, :], -jnp.inf)
        return diag if qi == 0 else jnp.concatenate([s[:kend - Q_TILE, :], diag], axis=0)

    def prompt_finish(qi, s):
        kend = (qi + 1) * Q_TILE
        p = jnp.exp(s - jnp.max(s, axis=0, keepdims=True))
        denom = jnp.sum(p, axis=0, keepdims=True)
        ot = jnp.dot(vt[:, :kend], p.astype(BF16), preferred_element_type=F32)
        o_ref[qi * Q_TILE:kend, :] = (ot / denom).T.astype(BF16)

    q_lat = qlat_ref[0]
    q_rope = qrope_ref[0]
    c_new = cnew_ref[0]
    kr_new = krnew_ref[0]

    def sample_scores(j):
        keys = slice(j * KEY_CHUNK, (j + 1) * KEY_CHUNK)
        cb = cbuf[slot, keys, :].astype(BF16)
        krt = krbuf[slot, :, keys].astype(BF16)
        s = (lax.dot_general(q_lat, cb, nt, preferred_element_type=F32)
             + jnp.dot(q_rope, krt, preferred_element_type=F32))
        return s, cb

    def sample_update(state, s, cb):
        m, denom, acc = state
        m_new = jnp.maximum(m, jnp.max(s, axis=1, keepdims=True))
        alpha = jnp.exp(m - m_new)
        p = jnp.exp(s - m_new)
        denom = denom * alpha + jnp.sum(p, axis=1, keepdims=True)
        acc = acc * alpha + jnp.dot(p.astype(BF16), cb, preferred_element_type=F32)
        return m_new, denom, acc

    m0 = (jnp.sum(q_lat.astype(F32) * c_new, axis=1, keepdims=True)
          + jnp.sum(q_rope.astype(F32) * kr_new, axis=1, keepdims=True))
    state = (m0, jnp.ones_like(m0), jnp.broadcast_to(c_new, (N_HEADS, KV_LORA_RANK)))

    per = n_q // n_chunks
    s_next = prompt_scores(0)
    sc_next = sample_scores(0)
    for qi in range(n_q):
        s_cur = s_next
        if qi + 1 < n_q:
            s_next = prompt_scores(qi + 1)
        prompt_finish(qi, s_cur)
        if qi % per == per - 1:
            j = qi // per
            sc_cur = sc_next
            if j + 1 < n_chunks:
                sc_next = sample_scores(j + 1)
            state = sample_update(state, *sc_cur)
    _, denom, acc = state
    olat_ref[0] = (acc / denom).astype(BF16)

    @pl.when(s_idx == n_seq - 1)
    def _():
        wait_pages(1 - slot)


def _attention(page_table, q_all, kn, krkr, v, q_lat, q_rope, c_new, kr_new, cache_c, cache_krt,
               batch, seq):
    n_seq, n_pages = page_table.shape
    page_size = cache_c.shape[1]
    past = n_pages * page_size
    assert n_seq == batch * N_HEADS and past % KEY_CHUNK == 0 and seq % Q_TILE == 0
    assert (seq // Q_TILE) % (past // KEY_CHUNK) == 0

    def per_seq(width):
        return pl.BlockSpec((1,) + width, lambda b, h, pt: (b * N_HEADS + h, 0, 0))

    grid_spec = pltpu.PrefetchScalarGridSpec(
        num_scalar_prefetch=1,
        grid=(batch, N_HEADS),
        in_specs=[pl.BlockSpec((seq, HEAD_DIM_PAD), lambda b, h, pt: (b, h)),
                  pl.BlockSpec((seq, QK_NOPE_DIM), lambda b, h, pt: (b, h)),
                  pl.BlockSpec((seq, LANES), lambda b, h, pt: (b, 0)),
                  pl.BlockSpec((seq, V_HEAD_DIM), lambda b, h, pt: (b, h)),
                  per_seq((N_HEADS, KV_LORA_RANK)), per_seq((N_HEADS, QK_ROPE_DIM)),
                  per_seq((1, KV_LORA_RANK)), per_seq((1, QK_ROPE_DIM)),
                  pl.BlockSpec(memory_space=pl.ANY), pl.BlockSpec(memory_space=pl.ANY)],
        out_specs=[pl.BlockSpec((seq, V_HEAD_DIM), lambda b, h, pt: (b, h)),
                   per_seq((N_HEADS, KV_LORA_RANK))],
        scratch_shapes=[pltpu.VMEM((2, past, KV_LORA_RANK), F32),
                        pltpu.VMEM((2, QK_ROPE_DIM, past), F32),
                        pltpu.SemaphoreType.DMA((2, 2))],
    )
    return pl.pallas_call(
        functools.partial(_attn_kernel, n_pages=n_pages, page_size=page_size),
        grid_spec=grid_spec,
        out_shape=[jax.ShapeDtypeStruct((batch * seq, N_HEADS * V_HEAD_DIM), BF16),
                   jax.ShapeDtypeStruct((n_seq, N_HEADS, KV_LORA_RANK), BF16)],
        compiler_params=_params(2),
        name="attention",
    )(page_table, q_all, kn, krkr, v, q_lat, q_rope, c_new, kr_new, cache_c, cache_krt)


def _rope_table(pos):
    inv = ROPE_THETA ** (-jnp.arange(0, QK_ROPE_DIM, 2, dtype=F32) / QK_ROPE_DIM)
    ang = pos[:, None] * inv[None, :]
    cos, sin = jnp.cos(ang), jnp.sin(ang)
    reps = LANES // QK_ROPE_DIM
    return jnp.concatenate([cos, cos] * reps + [-sin, sin] * reps, axis=1)


def kernel(x_prompt, x_sample, cache_kv_latent, cache_k_rope, page_table, norm_mix, sgu_w_in,
           sgu_v_norm, sgu_w_s, sgu_b_s, sgu_w_out, norm_ffn, ffn_w_in, ffn_w_out, kv_norm, w_dkv,
           kv_latent_norm, w_uk, w_uv, w_dq, q_norm, w_uq, w_o, final_norm):
    batch, seq, _ = x_prompt.shape
    n_sample, dec_seq, _ = x_sample.shape
    n_pages = page_table.shape[1]
    page_size = cache_kv_latent.shape[1]
    past_len = n_pages * page_size
    assert dec_seq == 1 and past_len % CHUNK == 0 and seq % TOKEN_TILE == 0
    assert n_sample == CHUNK
    n_a = sgu_w_in.shape[0]
    n_b = w_dq.shape[0]
    t_prompt = batch * seq

    bf = lambda a: a.astype(BF16)
    row = lambda a: a.reshape(1, -1)
    rot = jnp.concatenate([jnp.arange(QK_ROPE_DIM // 2, QK_ROPE_DIM),
                           jnp.arange(0, QK_ROPE_DIM // 2)])
    w_kr = w_dkv[:, KV_LORA_RANK:]
    w_dkv_ext = bf(jnp.concatenate([w_dkv[:, :KV_LORA_RANK], w_kr, w_kr,
                                    w_kr[:, rot], w_kr[:, rot]], axis=1))
    w_uk2 = bf(w_uk.reshape(KV_LORA_RANK, N_HEADS * QK_NOPE_DIM))
    w_uv2 = bf(w_uv.reshape(KV_LORA_RANK, N_HEADS * V_HEAD_DIM))
    w_ukt = bf(jnp.transpose(w_uk, (1, 2, 0)))
    w_uvt = bf(jnp.transpose(w_uv, (1, 0, 2)))
    w_uq3 = w_uq.reshape(n_b, Q_LORA_RANK, N_HEADS, QK_NOPE_DIM + QK_ROPE_DIM)
    w_uq_rope = w_uq3[..., QK_NOPE_DIM:]
    w_uq_ext = bf(jnp.concatenate(
        [w_uq3[..., :QK_NOPE_DIM].reshape(n_b, Q_LORA_RANK, -1),
         w_uq_rope.reshape(n_b, Q_LORA_RANK, -1),
         w_uq_rope[..., rot].reshape(n_b, Q_LORA_RANK, -1)], axis=2))
    tab_p = _rope_table(jnp.arange(seq, dtype=F32))
    tab_s = jnp.broadcast_to(_rope_table(jnp.full((1,), past_len, F32)), (n_sample, 2 * LANES))
    cache_krt = jnp.swapaxes(cache_k_rope, 1, 2)

    rows = lambda a: a.reshape(a.shape[0], 1, a.shape[1])
    norm_mix, norm_ffn, sgu_v_norm, q_norm = map(rows, (norm_mix, norm_ffn, sgu_v_norm, q_norm))
    sgu_w_in, sgu_w_out, ffn_w_in, ffn_w_out, w_dq, w_o = map(
        bf, (sgu_w_in, sgu_w_out, ffn_w_in, ffn_w_out, w_dq, w_o))
    sgu_b_t = jnp.swapaxes(sgu_b_s, 1, 2)

    hp = x_prompt.reshape(t_prompt, D_MODEL)
    hs = x_sample.reshape(n_sample, D_MODEL)
    v_rows = []
    for l in range(n_a):
        hp, hs, v_s = _sgu_layer(hp, hs, (norm_mix, l), (sgu_w_in, l), (sgu_v_norm, l),
                                 (sgu_w_s, l), (sgu_b_t, l), (sgu_w_out, l))
        v_rows.append(v_s)
        hp, hs = _ffn_layer(hp, hs, (norm_ffn, l), (ffn_w_in, l), (ffn_w_out, l))

    c_p, kr_p, kn, krkr, v, c_s, kr_s = _kv_layer(hp, hs, tab_p, tab_s, row(kv_norm), w_dkv_ext,
                                                  row(kv_latent_norm), w_uk2, w_uv2)
    c_new = c_s.reshape(n_sample, 1, KV_LORA_RANK)
    kr_new = kr_s.reshape(n_sample, 1, QK_ROPE_DIM)

    for j in range(n_b):
        l = n_a + j
        q_all, q_lat, q_rope = _q_layer(hp, hs, tab_p, tab_s, (norm_mix, l), (w_dq, j),
                                        (q_norm, j), (w_uq_ext, j), w_ukt)
        o_p, o_lat = _attention(page_table, q_all, kn, krkr, v,
                                q_lat.reshape(n_sample, N_HEADS, KV_LORA_RANK),
                                q_rope.reshape(n_sample, N_HEADS, QK_ROPE_DIM), c_new, kr_new,
                                cache_kv_latent, cache_krt, batch, seq)
        hp, hs = _ffn_layer(hp, hs, (norm_ffn, l), (ffn_w_in, l), (ffn_w_out, l),
                            attn=(o_p, o_lat.reshape(n_sample, N_HEADS * KV_LORA_RANK), w_uvt,
                                  (w_o, j)),
                            final=row(final_norm) if j == n_b - 1 else None)

    return (hp.reshape(batch, seq, D_MODEL),
            hs.reshape(n_sample, 1, D_MODEL),
            c_p.reshape(batch, seq, KV_LORA_RANK),
            kr_p.reshape(batch, seq, QK_ROPE_DIM),
            c_new,
            kr_new,
            jnp.stack(v_rows).reshape(n_a, n_sample, 1, D_SGU))
```

```python
import functools

import jax
import jax.numpy as jnp
from jax import lax
from jax.experimental import pallas as pl
from jax.experimental.pallas import tpu as pltpu

D_MODEL = 1024
CHUNK = 128
D_SGU = 2 * D_MODEL
SGU_GROUPS = 8
SGU_GROUP_DIM = D_SGU // SGU_GROUPS
N_HEADS = 16
QK_NOPE_DIM = 128
QK_ROPE_DIM = 64
V_HEAD_DIM = 128
Q_LORA_RANK = 384
KV_LORA_RANK = 256
ROPE_THETA = 10000.0
ATTN_SCALE = (QK_NOPE_DIM + QK_ROPE_DIM) ** -0.5
LOG2_E = 1.4426950408889634
EPS = 1e-6

LANES = 128
TOKEN_TILE = 512
FFN_COLS = 2 * LANES
HEAD_DIM_PAD = 2 * LANES
Q_TILE = 256
KEY_CHUNK = 2048
VMEM_LIMIT = 56 * 1024 * 1024

F32 = jnp.float32
BF16 = jnp.bfloat16


def _rms(x, g):
    return x * lax.rsqrt(jnp.mean(x * x, axis=-1, keepdims=True) + EPS) * g


def _rope_pairs(x, tab_ref):
    half = QK_ROPE_DIM // 2
    lane = lax.broadcasted_iota(jnp.int32, (1, LANES), 1)
    partner = jnp.where(lane % QK_ROPE_DIM < half,
                        pltpu.roll(x, LANES - half, 1), pltpu.roll(x, half, 1))
    return x * tab_ref[:, :LANES] + partner * tab_ref[:, LANES:]


def _const_spec(const):
    if isinstance(const, tuple):
        stack, layer = const
        zeros = (0,) * (stack.ndim - 1)
        return pl.BlockSpec((None,) + stack.shape[1:], lambda *_: (layer,) + zeros,
                            pipeline_mode=pl.Buffered(1))
    zeros = (0,) * const.ndim
    return pl.BlockSpec(const.shape, lambda *_: zeros, pipeline_mode=pl.Buffered(1))


def _params(n=1):
    return pltpu.CompilerParams(dimension_semantics=("arbitrary",) * n,
                                vmem_limit_bytes=VMEM_LIMIT)


def _two_stream_call(body, prompt_in, sample_in, consts, prompt_out, sample_out, *, name,
                     scratch=(), setup=None):
    t_prompt = prompt_in[0].shape[0]
    n_sample = sample_in[0].shape[0]
    n_tiles = t_prompt // TOKEN_TILE
    assert t_prompt % TOKEN_TILE == 0
    n_in, n_c = len(prompt_in), len(consts)
    prompt_out = [po for po in prompt_out if po is not None]
    sample_out = [so for so in sample_out if so is not None]
    n_po, n_so = len(prompt_out), len(sample_out)

    def kern(*refs):
        refs = list(refs)
        p_in, refs = refs[:n_in], refs[n_in:]
        s_in, refs = refs[:n_in], refs[n_in:]
        c_in, refs = refs[:n_c], refs[n_c:]
        p_out, refs = refs[:n_po], refs[n_po:]
        s_out, scr = refs[:n_so], refs[n_so:]
        i = pl.program_id(0)
        if setup is not None:
            pl.when(i == 0)(lambda: setup(c_in, scr))
        pl.when(i < n_tiles)(lambda: body(p_in, c_in, p_out, scr, False))
        pl.when(i == n_tiles)(lambda: body(s_in, c_in, s_out, scr, True))

    def p_spec(width, rows=None):
        if rows is None:
            return pl.BlockSpec((TOKEN_TILE, width), lambda i: (jnp.minimum(i, n_tiles - 1), 0))
        period = rows // TOKEN_TILE
        return pl.BlockSpec((TOKEN_TILE, width),
                            lambda i: (jnp.minimum(i, n_tiles - 1) % period, 0))

    def s_spec(rows, width):
        return pl.BlockSpec((rows, width), lambda i: (0, 0))

    in_specs = ([p_spec(a.shape[1], None if a.shape[0] == t_prompt else a.shape[0])
                 for a in prompt_in]
                + [s_spec(*a.shape) for a in sample_in]
                + [_const_spec(c) for c in consts])
    out_specs = ([p_spec(w) for w, _ in prompt_out] + [s_spec(n_sample, w) for w, _ in sample_out])
    out_shape = ([jax.ShapeDtypeStruct((t_prompt, w), d) for w, d in prompt_out]
                 + [jax.ShapeDtypeStruct((n_sample, w), d) for w, d in sample_out])
    return pl.pallas_call(
        kern, grid=(n_tiles + 1,), in_specs=in_specs, out_specs=out_specs, out_shape=out_shape,
        scratch_shapes=list(scratch), compiler_params=_params(), name=name,
    )(*prompt_in, *sample_in, *[c[0] if isinstance(c, tuple) else c for c in consts])


def _sgu_setup(consts, scratch):
    ws_ref, bt_ref = consts[3], consts[4]
    wmix_ref, bmix_ref = scratch
    r = lax.broadcasted_iota(jnp.int32, (CHUNK, CHUNK), 0)
    c = lax.broadcasted_iota(jnp.int32, (CHUNK, CHUNK), 1)
    for g in range(SGU_GROUPS):
        w = ws_ref[g]
        wmix_ref[0, g] = jnp.where(c <= r, w, 0.0).astype(BF16)
        wmix_ref[1, g] = jnp.where(c == r, w[0:1, 0:1], 0.0).astype(BF16)
    bmix_ref[0] = bt_ref[...]
    bmix_ref[1] = jnp.broadcast_to(bt_ref[0:1, :], bt_ref.shape)


def _sgu_body(ins, consts, outs, scratch, is_sample):
    (h_ref,) = ins
    g_ref, win_ref, vn_ref, _, _, wout_ref = consts
    wmix_ref, bmix_ref = scratch
    sel = 1 if is_sample else 0
    x = h_ref[...]
    hn = _rms(x, g_ref[...]).astype(BF16)
    z = jax.nn.gelu(jnp.dot(hn, win_ref[...], preferred_element_type=F32))
    u = z[:, :D_SGU]
    v = _rms(z[:, D_SGU:], vn_ref[...])
    if is_sample:
        outs[1][...] = v
    vb = v.astype(BF16)
    bias = bmix_ref[sel]
    ys = []
    for c in range(x.shape[0] // CHUNK):
        rows = slice(c * CHUNK, (c + 1) * CHUNK)
        parts = []
        for g in range(SGU_GROUPS):
            cols = slice(g * SGU_GROUP_DIM, (g + 1) * SGU_GROUP_DIM)
            m = jnp.dot(wmix_ref[sel, g], vb[rows, cols], preferred_element_type=F32)
            parts.append(m + bias[:, g:g + 1])
        ys.append((u[rows] * jnp.concatenate(parts, axis=1)).astype(BF16))
    y = ys[0] if len(ys) == 1 else jnp.concatenate(ys, axis=0)
    outs[0][...] = x + jnp.dot(y, wout_ref[...], preferred_element_type=F32)


def _sgu_layer(hp, hs, g, w_in, v_norm, w_s, b_t, w_out):
    return _two_stream_call(
        _sgu_body, [hp], [hs], [g, w_in, v_norm, w_s, b_t, w_out],
        [(D_MODEL, F32), None], [(D_MODEL, F32), (D_SGU, F32)], name="sgu_mixer", setup=_sgu_setup,
        scratch=[pltpu.VMEM((2, SGU_GROUPS, CHUNK, CHUNK), BF16),
                 pltpu.VMEM((2, CHUNK, SGU_GROUPS), F32)])


def _ffn_body(ins, consts, outs, scratch, is_sample, *, d_ff, n_split, has_attn, has_final):
    del scratch
    consts = list(consts)
    x = ins[0][...]
    if has_attn:
        wuv_ref, wo_ref = consts.pop(0), consts.pop(0)
        if is_sample:
            o = jnp.concatenate(
                [jnp.dot(ins[1][:, hd * KV_LORA_RANK:(hd + 1) * KV_LORA_RANK], wuv_ref[hd],
                         preferred_element_type=F32).astype(BF16) for hd in range(N_HEADS)], axis=1)
        else:
            o = ins[1][...]
        x = x + jnp.dot(o, wo_ref[...], preferred_element_type=F32)
    g_ref, win_ref, wout_ref = consts[:3]
    hn = _rms(x, g_ref[...]).astype(BF16)
    width = d_ff // n_split

    def gate_up(j):
        return (jnp.dot(hn, win_ref[:, j * width:(j + 1) * width], preferred_element_type=F32),
                jnp.dot(hn, win_ref[:, d_ff + j * width:d_ff + (j + 1) * width],
                        preferred_element_type=F32))

    acts = []
    nxt = gate_up(0)
    for j in range(n_split):
        gate, up = nxt
        if j + 1 < n_split:
            nxt = gate_up(j + 1)
        acts.append((jax.nn.silu(gate) * up).astype(BF16))
    acc = x + jnp.dot(jnp.concatenate(acts, axis=1), wout_ref[...], preferred_element_type=F32)
    if has_final:
        acc = _rms(acc, consts[3][...])
    outs[0][...] = acc


def _ffn_layer(hp, hs, g, w_in, w_out, attn=None, final=None):
    prompt_in, sample_in, consts = [hp], [hs], []
    if attn is not None:
        o_prompt, o_lat_sample, w_uvt, w_o = attn
        prompt_in.append(o_prompt)
        sample_in.append(o_lat_sample)
        consts += [w_uvt, w_o]
    consts += [g, w_in, w_out]
    if final is not None:
        consts.append(final)
    d_ff = w_out[0].shape[-2]
    assert d_ff % FFN_COLS == 0
    body = functools.partial(_ffn_body, d_ff=d_ff, n_split=d_ff // FFN_COLS,
                             has_attn=attn is not None, has_final=final is not None)
    return _two_stream_call(body, prompt_in, sample_in, consts, [(D_MODEL, F32)], [(D_MODEL, F32)],
                            name="ffn")


def _kv_body(ins, consts, outs, scratch, is_sample):
    del scratch
    h_ref, tab_ref = ins
    g_ref, wdkv_ref, ln_ref, wuk_ref, wuv_ref = consts
    hn = _rms(h_ref[...], g_ref[...]).astype(BF16)
    ckr = jnp.dot(hn, wdkv_ref[...], preferred_element_type=F32)
    c = _rms(ckr[:, :KV_LORA_RANK], ln_ref[...])
    krkr = _rope_pairs(ckr[:, KV_LORA_RANK:], tab_ref)
    outs[0][...] = c
    outs[1][...] = krkr[:, :QK_ROPE_DIM]
    if not is_sample:
        cb = c.astype(BF16)
        outs[2][...] = jnp.dot(cb, wuk_ref[...], preferred_element_type=F32).astype(BF16)
        outs[3][...] = krkr.astype(BF16)
        outs[4][...] = jnp.dot(cb, wuv_ref[...], preferred_element_type=F32).astype(BF16)


def _kv_layer(hp, hs, tab_p, tab_s, g, w_dkv_ext, lat_norm, w_uk2, w_uv2):
    return _two_stream_call(
        _kv_body, [hp, tab_p], [hs, tab_s], [g, w_dkv_ext, lat_norm, w_uk2, w_uv2],
        [(KV_LORA_RANK, F32), (QK_ROPE_DIM, F32), (N_HEADS * QK_NOPE_DIM, BF16), (LANES, BF16),
         (N_HEADS * V_HEAD_DIM, BF16)],
        [(KV_LORA_RANK, F32), (QK_ROPE_DIM, F32), None, None, None], name="shared_kv")


def _q_body(ins, consts, outs, scratch, is_sample):
    del scratch
    h_ref, tab_ref = ins
    g_ref, wdq_ref, qn_ref, wuq_ref, wukt_ref = consts
    hn = _rms(h_ref[...], g_ref[...]).astype(BF16)
    cq = _rms(jnp.dot(hn, wdq_ref[...], preferred_element_type=F32), qn_ref[...]).astype(BF16)
    z = jnp.dot(cq, wuq_ref[...], preferred_element_type=F32) * (ATTN_SCALE * LOG2_E)
    nope_w = N_HEADS * QK_NOPE_DIM
    low = lax.broadcasted_iota(jnp.int32, (1, LANES), 1) < QK_ROPE_DIM
    for t in range(N_HEADS // 2):
        rt = _rope_pairs(z[:, nope_w + t * LANES:nope_w + (t + 1) * LANES], tab_ref)
        if is_sample:
            outs[1][:, t * LANES:(t + 1) * LANES] = rt.astype(BF16)
            continue
        for half in range(2):
            hd = 2 * t + half
            base = hd * HEAD_DIM_PAD
            outs[0][:, base:base + LANES] = z[:, hd * LANES:(hd + 1) * LANES].astype(BF16)
            keep = low if half == 0 else jnp.logical_not(low)
            outs[0][:, base + LANES:base + 2 * LANES] = jnp.where(keep, rt, 0.0).astype(BF16)
    if is_sample:
        for hd in range(N_HEADS):
            qn = z[:, hd * LANES:(hd + 1) * LANES].astype(BF16)
            outs[0][:, hd * KV_LORA_RANK:(hd + 1) * KV_LORA_RANK] = jnp.dot(
                qn, wukt_ref[hd], preferred_element_type=F32).astype(BF16)


def _q_layer(hp, hs, tab_p, tab_s, g, w_dq, q_norm, w_uq_ext, w_ukt):
    return _two_stream_call(
        _q_body, [hp, tab_p], [hs, tab_s], [g, w_dq, q_norm, w_uq_ext, w_ukt],
        [(N_HEADS * HEAD_DIM_PAD, BF16), None],
        [(N_HEADS * KV_LORA_RANK, BF16), (N_HEADS * QK_ROPE_DIM, BF16)], name="mla_query")


def _attn_kernel(pt_ref, q_ref, kn_ref, krkr_ref, v_ref, qlat_ref, qrope_ref, cnew_ref, krnew_ref,
                 cc_ref, ckr_ref, o_ref, olat_ref, cbuf, krbuf, sems, *, n_pages, page_size):
    n_seq = pl.num_programs(0) * pl.num_programs(1)
    s_idx = pl.program_id(0) * pl.num_programs(1) + pl.program_id(1)
    slot = s_idx % 2

    def start_pages(seq, sl, pages):
        for p in pages:
            page = pt_ref[seq, p]
            keys = pl.ds(p * page_size, page_size)
            pltpu.make_async_copy(cc_ref.at[page], cbuf.at[sl, keys], sems.at[0, sl]).start()
            pltpu.make_async_copy(ckr_ref.at[page], krbuf.at[sl, :, keys], sems.at[1, sl]).start()

    def wait_pages(sl):
        pltpu.make_async_copy(cbuf.at[sl], cbuf.at[sl], sems.at[0, sl]).wait()
        pltpu.make_async_copy(krbuf.at[sl], krbuf.at[sl], sems.at[1, sl]).wait()

    @pl.when(s_idx == 0)
    def _():
        start_pages(0, 0, range(n_pages))

    wait_pages(slot)
    start_pages(jnp.minimum(s_idx + 1, n_seq - 1), 1 - slot, range(n_pages))

    seq = q_ref.shape[0]
    k = jnp.concatenate([kn_ref[...], krkr_ref[...]], axis=1)
    vt = v_ref[...].T
    row = lax.broadcasted_iota(jnp.int32, (Q_TILE, Q_TILE), 0)
    col = lax.broadcasted_iota(jnp.int32, (Q_TILE, Q_TILE), 1)
    nt = (((1,), (1,)), ((), ()))
    n_q = seq // Q_TILE
    n_chunks = n_pages * page_size // KEY_CHUNK

    def prompt_scores(qi):
        kend = (qi + 1) * Q_TILE
        q = q_ref[qi * Q_TILE:kend, :]
        s = lax.dot_general(k[:kend], q, nt, preferred_element_type=F32)
        diag = jnp.where(row <= col, s[kend - Q_TILE:, :], -jnp.inf)
        return diag if qi == 0 else jnp.concatenate([s[:kend - Q_TILE, :], diag], axis=0)

    def prompt_finish(qi, s):
        kend = (qi + 1) * Q_TILE
        p = jnp.exp2(s - jnp.max(s, axis=0, keepdims=True))
        denom = jnp.sum(p, axis=0, keepdims=True)
        ot = jnp.dot(vt[:, :kend], p.astype(BF16), preferred_element_type=F32)
        o_ref[qi * Q_TILE:kend, :] = (ot / denom).T.astype(BF16)

    q_lat = qlat_ref[0]
    q_rope = qrope_ref[0]
    c_new = cnew_ref[0]
    kr_new = krnew_ref[0]

    def sample_scores(j):
        keys = slice(j * KEY_CHUNK, (j + 1) * KEY_CHUNK)
        cb = cbuf[slot, keys, :].astype(BF16)
        krt = krbuf[slot, :, keys].astype(BF16)
        s = (lax.dot_general(q_lat, cb, nt, preferred_element_type=F32)
             + jnp.dot(q_rope, krt, preferred_element_type=F32))
        return s, cb

    def sample_update(state, s, cb):
        m, denom, acc = state
        m_new = jnp.maximum(m, jnp.max(s, axis=1, keepdims=True))
        alpha = jnp.exp2(m - m_new)
        p = jnp.exp2(s - m_new)
        denom = denom * alpha + jnp.sum(p, axis=1, keepdims=True)
        acc = acc * alpha + jnp.dot(p.astype(BF16), cb, preferred_element_type=F32)
        return m_new, denom, acc

    m0 = (jnp.sum(q_lat.astype(F32) * c_new, axis=1, keepdims=True)
          + jnp.sum(q_rope.astype(F32) * kr_new, axis=1, keepdims=True))
    state = (m0, jnp.ones_like(m0), jnp.broadcast_to(c_new, (N_HEADS, KV_LORA_RANK)))

    per = n_q // n_chunks
    s_next = prompt_scores(0)
    sc_next = sample_scores(0)
    for qi in range(n_q):
        s_cur = s_next
        if qi + 1 < n_q:
            s_next = prompt_scores(qi + 1)
        prompt_finish(qi, s_cur)
        if qi % per == per - 1:
            j = qi // per
            sc_cur = sc_next
            if j + 1 < n_chunks:
                sc_next = sample_scores(j + 1)
            state = sample_update(state, *sc_cur)
    _, denom, acc = state
    olat_ref[0] = (acc / denom).astype(BF16)

    @pl.when(s_idx == n_seq - 1)
    def _():
        wait_pages(1 - slot)


def _attention(page_table, q_all, kn, krkr, v, q_lat, q_rope, c_new, kr_new, cache_c, cache_krt,
               batch, seq):
    n_seq, n_pages = page_table.shape
    page_size = cache_c.shape[1]
    past = n_pages * page_size
    assert n_seq == batch * N_HEADS and past % KEY_CHUNK == 0 and seq % Q_TILE == 0
    assert (seq // Q_TILE) % (past // KEY_CHUNK) == 0

    def per_seq(width):
        return pl.BlockSpec((1,) + width, lambda b, h, pt: (b * N_HEADS + h, 0, 0))

    grid_spec = pltpu.PrefetchScalarGridSpec(
        num_scalar_prefetch=1,
        grid=(batch, N_HEADS),
        in_specs=[pl.BlockSpec((seq, HEAD_DIM_PAD), lambda b, h, pt: (b, h)),
                  pl.BlockSpec((seq, QK_NOPE_DIM), lambda b, h, pt: (b, h)),
                  pl.BlockSpec((seq, LANES), lambda b, h, pt: (b, 0)),
                  pl.BlockSpec((seq, V_HEAD_DIM), lambda b, h, pt: (b, h)),
                  per_seq((N_HEADS, KV_LORA_RANK)), per_seq((N_HEADS, QK_ROPE_DIM)),
                  per_seq((1, KV_LORA_RANK)), per_seq((1, QK_ROPE_DIM)),
                  pl.BlockSpec(memory_space=pl.ANY), pl.BlockSpec(memory_space=pl.ANY)],
        out_specs=[pl.BlockSpec((seq, V_HEAD_DIM), lambda b, h, pt: (b, h)),
                   per_seq((N_HEADS, KV_LORA_RANK))],
        scratch_shapes=[pltpu.VMEM((2, past, KV_LORA_RANK), F32),
                        pltpu.VMEM((2, QK_ROPE_DIM, past), F32),
                        pltpu.SemaphoreType.DMA((2, 2))],
    )
    return pl.pallas_call(
        functools.partial(_attn_kernel, n_pages=n_pages, page_size=page_size),
        grid_spec=grid_spec,
        out_shape=[jax.ShapeDtypeStruct((batch * seq, N_HEADS * V_HEAD_DIM), BF16),
                   jax.ShapeDtypeStruct((n_seq, N_HEADS, KV_LORA_RANK), BF16)],
        compiler_params=_params(2),
        name="attention",
    )(page_table, q_all, kn, krkr, v, q_lat, q_rope, c_new, kr_new, cache_c, cache_krt)


def _rope_table(pos):
    inv = ROPE_THETA ** (-jnp.arange(0, QK_ROPE_DIM, 2, dtype=F32) / QK_ROPE_DIM)
    ang = pos[:, None] * inv[None, :]
    cos, sin = jnp.cos(ang), jnp.sin(ang)
    reps = LANES // QK_ROPE_DIM
    return jnp.concatenate([cos, cos] * reps + [-sin, sin] * reps, axis=1)


def kernel(x_prompt, x_sample, cache_kv_latent, cache_k_rope, page_table, norm_mix, sgu_w_in,
           sgu_v_norm, sgu_w_s, sgu_b_s, sgu_w_out, norm_ffn, ffn_w_in, ffn_w_out, kv_norm, w_dkv,
           kv_latent_norm, w_uk, w_uv, w_dq, q_norm, w_uq, w_o, final_norm):
    batch, seq, _ = x_prompt.shape
    n_sample, dec_seq, _ = x_sample.shape
    n_pages = page_table.shape[1]
    page_size = cache_kv_latent.shape[1]
    past_len = n_pages * page_size
    assert dec_seq == 1 and past_len % CHUNK == 0 and seq % TOKEN_TILE == 0
    assert n_sample == CHUNK
    n_a = sgu_w_in.shape[0]
    n_b = w_dq.shape[0]
    t_prompt = batch * seq

    bf = lambda a: a.astype(BF16)
    row = lambda a: a.reshape(1, -1)
    w_dkv_ext = bf(jnp.concatenate([w_dkv, w_dkv[:, KV_LORA_RANK:]], axis=1))
    w_uk2 = bf(w_uk.reshape(KV_LORA_RANK, N_HEADS * QK_NOPE_DIM))
    w_uv2 = bf(w_uv.reshape(KV_LORA_RANK, N_HEADS * V_HEAD_DIM))
    w_ukt = bf(jnp.transpose(w_uk, (1, 2, 0)))
    w_uvt = bf(jnp.transpose(w_uv, (1, 0, 2)))
    w_uq3 = w_uq.reshape(n_b, Q_LORA_RANK, N_HEADS, QK_NOPE_DIM + QK_ROPE_DIM)
    w_uq_ext = bf(jnp.concatenate(
        [w_uq3[..., :QK_NOPE_DIM].reshape(n_b, Q_LORA_RANK, -1),
         w_uq3[..., QK_NOPE_DIM:].reshape(n_b, Q_LORA_RANK, -1)], axis=2))
    tab_p = _rope_table(jnp.arange(seq, dtype=F32))
    tab_s = jnp.broadcast_to(_rope_table(jnp.full((1,), past_len, F32)), (n_sample, 2 * LANES))
    cache_krt = jnp.swapaxes(cache_k_rope, 1, 2)

    rows = lambda a: a.reshape(a.shape[0], 1, a.shape[1])
    norm_mix, norm_ffn, sgu_v_norm, q_norm = map(rows, (norm_mix, norm_ffn, sgu_v_norm, q_norm))
    sgu_w_in, sgu_w_out, ffn_w_in, ffn_w_out, w_dq, w_o = map(
        bf, (sgu_w_in, sgu_w_out, ffn_w_in, ffn_w_out, w_dq, w_o))
    sgu_b_t = jnp.swapaxes(sgu_b_s, 1, 2)

    hp = x_prompt.reshape(t_prompt, D_MODEL)
    hs = x_sample.reshape(n_sample, D_MODEL)
    v_rows = []
    for l in range(n_a):
        hp, hs, v_s = _sgu_layer(hp, hs, (norm_mix, l), (sgu_w_in, l), (sgu_v_norm, l),
                                 (sgu_w_s, l), (sgu_b_t, l), (sgu_w_out, l))
        v_rows.append(v_s)
        hp, hs = _ffn_layer(hp, hs, (norm_ffn, l), (ffn_w_in, l), (ffn_w_out, l))

    c_p, kr_p, kn, krkr, v, c_s, kr_s = _kv_layer(hp, hs, tab_p, tab_s, row(kv_norm), w_dkv_ext,
                                                  row(kv_latent_norm), w_uk2, w_uv2)
    c_new = c_s.reshape(n_sample, 1, KV_LORA_RANK)
    kr_new = kr_s.reshape(n_sample, 1, QK_ROPE_DIM)

    for j in range(n_b):
        l = n_a + j
        q_all, q_lat, q_rope = _q_layer(hp, hs, tab_p, tab_s, (norm_mix, l), (w_dq, j),
                                        (q_norm, j), (w_uq_ext, j), w_ukt)
        o_p, o_lat = _attention(page_table, q_all, kn, krkr, v,
                                q_lat.reshape(n_sample, N_HEADS, KV_LORA_RANK),
                                q_rope.reshape(n_sample, N_HEADS, QK_ROPE_DIM), c_new, kr_new,
                                cache_kv_latent, cache_krt, batch, seq)
        hp, hs = _ffn_layer(hp, hs, (norm_ffn, l), (ffn_w_in, l), (ffn_w_out, l),
                            attn=(o_p, o_lat.reshape(n_sample, N_HEADS * KV_LORA_RANK), w_uvt,
                                  (w_o, j)),
                            final=row(final_norm) if j == n_b - 1 else None)

    return (hp.reshape(batch, seq, D_MODEL),
            hs.reshape(n_sample, 1, D_MODEL),
            c_p.reshape(batch, seq, KV_LORA_RANK),
            kr_p.reshape(batch, seq, QK_ROPE_DIM),
            c_new,
            kr_new,
            jnp.stack(v_rows).reshape(n_a, n_sample, 1, D_SGU))
```

```python
import functools

import jax
import jax.numpy as jnp
from jax import lax
from jax.experimental import pallas as pl
from jax.experimental.pallas import tpu as pltpu

D_MODEL = 1024
CHUNK = 128
D_SGU = 2 * D_MODEL
SGU_GROUPS = 8
SGU_GROUP_DIM = D_SGU // SGU_GROUPS
N_HEADS = 16
QK_NOPE_DIM = 128
QK_ROPE_DIM = 64
V_HEAD_DIM = 128
Q_LORA_RANK = 384
KV_LORA_RANK = 256
ROPE_THETA = 10000.0
ATTN_SCALE = (QK_NOPE_DIM + QK_ROPE_DIM) ** -0.5
EPS = 1e-6

LANES = 128
TOKEN_TILE = 512
FFN_COLS = 2 * LANES
HEAD_DIM_PAD = 2 * LANES
Q_TILE = 256
SCORE_LOOKAHEAD = 2
KEY_CHUNK = 1024
VMEM_LIMIT = 56 * 1024 * 1024

F32 = jnp.float32
BF16 = jnp.bfloat16


def _rms(x, g):
    return x * lax.rsqrt(jnp.mean(x * x, axis=-1, keepdims=True) + EPS) * g


def _rope_pairs(x, tab_ref):
    half = QK_ROPE_DIM // 2
    lane = lax.broadcasted_iota(jnp.int32, (1, LANES), 1)
    partner = jnp.where(lane % QK_ROPE_DIM < half,
                        pltpu.roll(x, LANES - half, 1), pltpu.roll(x, half, 1))
    return x * tab_ref[:, :LANES] + partner * tab_ref[:, LANES:]


def _const_spec(const):
    if isinstance(const, tuple):
        stack, layer = const
        zeros = (0,) * (stack.ndim - 1)
        return pl.BlockSpec((None,) + stack.shape[1:], lambda *_: (layer,) + zeros,
                            pipeline_mode=pl.Buffered(1))
    zeros = (0,) * const.ndim
    return pl.BlockSpec(const.shape, lambda *_: zeros, pipeline_mode=pl.Buffered(1))


def _params(n=1):
    return pltpu.CompilerParams(dimension_semantics=("arbitrary",) * n,
                                vmem_limit_bytes=VMEM_LIMIT)


def _two_stream_call(body, prompt_in, sample_in, consts, prompt_out, sample_out, *, name,
                     scratch=(), setup=None):
    t_prompt = prompt_in[0].shape[0]
    n_sample = sample_in[0].shape[0]
    n_tiles = t_prompt // TOKEN_TILE
    assert t_prompt % TOKEN_TILE == 0
    n_in, n_c = len(prompt_in), len(consts)
    prompt_out = [po for po in prompt_out if po is not None]
    sample_out = [so for so in sample_out if so is not None]
    n_po, n_so = len(prompt_out), len(sample_out)

    def kern(*refs):
        refs = list(refs)
        p_in, refs = refs[:n_in], refs[n_in:]
        s_in, refs = refs[:n_in], refs[n_in:]
        c_in, refs = refs[:n_c], refs[n_c:]
        p_out, refs = refs[:n_po], refs[n_po:]
        s_out, scr = refs[:n_so], refs[n_so:]
        i = pl.program_id(0)
        if setup is not None:
            pl.when(i == 0)(lambda: setup(c_in, scr))
        pl.when(i < n_tiles)(lambda: body(p_in, c_in, p_out, scr, False))
        pl.when(i == n_tiles)(lambda: body(s_in, c_in, s_out, scr, True))

    def p_spec(width, rows=None):
        if rows is None:
            return pl.BlockSpec((TOKEN_TILE, width), lambda i: (jnp.minimum(i, n_tiles - 1), 0))
        period = rows // TOKEN_TILE
        return pl.BlockSpec((TOKEN_TILE, width),
                            lambda i: (jnp.minimum(i, n_tiles - 1) % period, 0))

    def s_spec(rows, width):
        return pl.BlockSpec((rows, width), lambda i: (0, 0))

    in_specs = ([p_spec(a.shape[1], None if a.shape[0] == t_prompt else a.shape[0])
                 for a in prompt_in]
                + [s_spec(*a.shape) for a in sample_in]
                + [_const_spec(c) for c in consts])
    out_specs = ([p_spec(w) for w, _ in prompt_out] + [s_spec(n_sample, w) for w, _ in sample_out])
    out_shape = ([jax.ShapeDtypeStruct((t_prompt, w), d) for w, d in prompt_out]
                 + [jax.ShapeDtypeStruct((n_sample, w), d) for w, d in sample_out])
    return pl.pallas_call(
        kern, grid=(n_tiles + 1,), in_specs=in_specs, out_specs=out_specs, out_shape=out_shape,
        scratch_shapes=list(scratch), compiler_params=_params(), name=name,
    )(*prompt_in, *sample_in, *[c[0] if isinstance(c, tuple) else c for c in consts])


def _sgu_setup(consts, scratch):
    ws_ref, bt_ref = consts[3], consts[4]
    wmix_ref, bmix_ref = scratch
    r = lax.broadcasted_iota(jnp.int32, (CHUNK, CHUNK), 0)
    c = lax.broadcasted_iota(jnp.int32, (CHUNK, CHUNK), 1)
    for g in range(SGU_GROUPS):
        w = ws_ref[g]
        wmix_ref[0, g] = jnp.where(c <= r, w, 0.0).astype(BF16)
        wmix_ref[1, g] = jnp.where(c == r, w[0:1, 0:1], 0.0).astype(BF16)
    bmix_ref[0] = bt_ref[...]
    bmix_ref[1] = jnp.broadcast_to(bt_ref[0:1, :], bt_ref.shape)


def _sgu_body(ins, consts, outs, scratch, is_sample):
    (h_ref,) = ins
    g_ref, win_ref, vn_ref, _, _, wout_ref = consts
    wmix_ref, bmix_ref = scratch
    sel = 1 if is_sample else 0
    x = h_ref[...]
    hn = _rms(x, g_ref[...]).astype(BF16)
    z = jax.nn.gelu(jnp.dot(hn, win_ref[...], preferred_element_type=F32))
    u = z[:, :D_SGU]
    v = _rms(z[:, D_SGU:], vn_ref[...])
    if is_sample:
        outs[1][...] = v
    vb = v.astype(BF16)
    bias = bmix_ref[sel]
    ys = []
    for c in range(x.shape[0] // CHUNK):
        rows = slice(c * CHUNK, (c + 1) * CHUNK)
        parts = []
        for g in range(SGU_GROUPS):
            cols = slice(g * SGU_GROUP_DIM, (g + 1) * SGU_GROUP_DIM)
            m = jnp.dot(wmix_ref[sel, g], vb[rows, cols], preferred_element_type=F32)
            parts.append(m + bias[:, g:g + 1])
        ys.append((u[rows] * jnp.concatenate(parts, axis=1)).astype(BF16))
    y = ys[0] if len(ys) == 1 else jnp.concatenate(ys, axis=0)
    outs[0][...] = x + jnp.dot(y, wout_ref[...], preferred_element_type=F32)


def _sgu_layer(hp, hs, g, w_in, v_norm, w_s, b_t, w_out):
    return _two_stream_call(
        _sgu_body, [hp], [hs], [g, w_in, v_norm, w_s, b_t, w_out],
        [(D_MODEL, F32), None], [(D_MODEL, F32), (D_SGU, F32)], name="sgu_mixer", setup=_sgu_setup,
        scratch=[pltpu.VMEM((2, SGU_GROUPS, CHUNK, CHUNK), BF16),
                 pltpu.VMEM((2, CHUNK, SGU_GROUPS), F32)])


def _ffn_body(ins, consts, outs, scratch, is_sample, *, d_ff, n_split, has_attn, has_final):
    del scratch
    consts = list(consts)
    x = ins[0][...]
    if has_attn:
        wuv_ref, wo_ref = consts.pop(0), consts.pop(0)
        if is_sample:
            o = jnp.concatenate(
                [jnp.dot(ins[1][:, hd * KV_LORA_RANK:(hd + 1) * KV_LORA_RANK], wuv_ref[hd],
                         preferred_element_type=F32).astype(BF16) for hd in range(N_HEADS)], axis=1)
        else:
            o = ins[1][...]
        x = x + jnp.dot(o, wo_ref[...], preferred_element_type=F32)
    g_ref, win_ref, wout_ref = consts[:3]
    hn = _rms(x, g_ref[...]).astype(BF16)
    width = d_ff // n_split

    def gate_up(j):
        return (jnp.dot(hn, win_ref[:, j * width:(j + 1) * width], preferred_element_type=F32),
                jnp.dot(hn, win_ref[:, d_ff + j * width:d_ff + (j + 1) * width],
                        preferred_element_type=F32))

    acts = []
    nxt = gate_up(0)
    for j in range(n_split):
        gate, up = nxt
        if j + 1 < n_split:
            nxt = gate_up(j + 1)
        acts.append((jax.nn.silu(gate) * up).astype(BF16))
    acc = x + jnp.dot(jnp.concatenate(acts, axis=1), wout_ref[...], preferred_element_type=F32)
    if has_final:
        acc = _rms(acc, consts[3][...])
    outs[0][...] = acc


def _ffn_layer(hp, hs, g, w_in, w_out, attn=None, final=None):
    prompt_in, sample_in, consts = [hp], [hs], []
    if attn is not None:
        o_prompt, o_lat_sample, w_uvt, w_o = attn
        prompt_in.append(o_prompt)
        sample_in.append(o_lat_sample)
        consts += [w_uvt, w_o]
    consts += [g, w_in, w_out]
    if final is not None:
        consts.append(final)
    d_ff = w_out[0].shape[-2]
    assert d_ff % FFN_COLS == 0
    body = functools.partial(_ffn_body, d_ff=d_ff, n_split=d_ff // FFN_COLS,
                             has_attn=attn is not None, has_final=final is not None)
    return _two_stream_call(body, prompt_in, sample_in, consts, [(D_MODEL, F32)], [(D_MODEL, F32)],
                            name="ffn")


def _kv_body(ins, consts, outs, scratch, is_sample):
    del scratch
    h_ref, tab_ref = ins
    g_ref, wdkv_ref, ln_ref, wuk_ref, wuv_ref = consts
    hn = _rms(h_ref[...], g_ref[...]).astype(BF16)
    ckr = jnp.dot(hn, wdkv_ref[...], preferred_element_type=F32)
    c = _rms(ckr[:, :KV_LORA_RANK], ln_ref[...])
    krkr = _rope_pairs(ckr[:, KV_LORA_RANK:], tab_ref)
    outs[0][...] = c
    outs[1][...] = krkr[:, :QK_ROPE_DIM]
    if not is_sample:
        cb = c.astype(BF16)
        outs[2][...] = jnp.dot(cb, wuk_ref[...], preferred_element_type=F32).astype(BF16)
        outs[3][...] = krkr.astype(BF16)
        outs[4][...] = jnp.dot(cb, wuv_ref[...], preferred_element_type=F32).astype(BF16)


def _kv_layer(hp, hs, tab_p, tab_s, g, w_dkv_ext, lat_norm, w_uk2, w_uv2):
    return _two_stream_call(
        _kv_body, [hp, tab_p], [hs, tab_s], [g, w_dkv_ext, lat_norm, w_uk2, w_uv2],
        [(KV_LORA_RANK, F32), (QK_ROPE_DIM, F32), (N_HEADS * QK_NOPE_DIM, BF16), (LANES, BF16),
         (N_HEADS * V_HEAD_DIM, BF16)],
        [(KV_LORA_RANK, F32), (QK_ROPE_DIM, F32), None, None, None], name="shared_kv")


def _q_body(ins, consts, outs, scratch, is_sample):
    del scratch
    h_ref, tab_ref = ins
    g_ref, wdq_ref, qn_ref, wuq_ref, wukt_ref = consts
    hn = _rms(h_ref[...], g_ref[...]).astype(BF16)
    cq = _rms(jnp.dot(hn, wdq_ref[...], preferred_element_type=F32), qn_ref[...]).astype(BF16)
    z = jnp.dot(cq, wuq_ref[...], preferred_element_type=F32) * ATTN_SCALE
    nope_w = N_HEADS * QK_NOPE_DIM
    low = lax.broadcasted_iota(jnp.int32, (1, LANES), 1) < QK_ROPE_DIM
    for t in range(N_HEADS // 2):
        rt = _rope_pairs(z[:, nope_w + t * LANES:nope_w + (t + 1) * LANES], tab_ref)
        if is_sample:
            outs[1][:, t * LANES:(t + 1) * LANES] = rt.astype(BF16)
            continue
        for half in range(2):
            hd = 2 * t + half
            base = hd * HEAD_DIM_PAD
            outs[0][:, base:base + LANES] = z[:, hd * LANES:(hd + 1) * LANES].astype(BF16)
            keep = low if half == 0 else jnp.logical_not(low)
            outs[0][:, base + LANES:base + 2 * LANES] = jnp.where(keep, rt, 0.0).astype(BF16)
    if is_sample:
        for hd in range(N_HEADS):
            qn = z[:, hd * LANES:(hd + 1) * LANES].astype(BF16)
            outs[0][:, hd * KV_LORA_RANK:(hd + 1) * KV_LORA_RANK] = jnp.dot(
                qn, wukt_ref[hd], preferred_element_type=F32).astype(BF16)


def _q_layer(hp, hs, tab_p, tab_s, g, w_dq, q_norm, w_uq_ext, w_ukt):
    return _two_stream_call(
        _q_body, [hp, tab_p], [hs, tab_s], [g, w_dq, q_norm, w_uq_ext, w_ukt],
        [(N_HEADS * HEAD_DIM_PAD, BF16), None],
        [(N_HEADS * KV_LORA_RANK, BF16), (N_HEADS * QK_ROPE_DIM, BF16)], name="mla_query")


def _attn_kernel(pt_ref, q_ref, kn_ref, krkr_ref, v_ref, qlat_ref, qrope_ref, cnew_ref, krnew_ref,
                 cc_ref, ckr_ref, o_ref, olat_ref, cbuf, krbuf, sems, *, n_pages, page_size):
    n_seq = pl.num_programs(0) * pl.num_programs(1)
    s_idx = pl.program_id(0) * pl.num_programs(1) + pl.program_id(1)
    slot = s_idx % 2

    def start_pages(seq, sl, pages):
        for p in pages:
            page = pt_ref[seq, p]
            keys = pl.ds(p * page_size, page_size)
            pltpu.make_async_copy(cc_ref.at[page], cbuf.at[sl, keys], sems.at[0, sl]).start()
            pltpu.make_async_copy(ckr_ref.at[page], krbuf.at[sl, :, keys], sems.at[1, sl]).start()

    def wait_pages(sl):
        pltpu.make_async_copy(cbuf.at[sl], cbuf.at[sl], sems.at[0, sl]).wait()
        pltpu.make_async_copy(krbuf.at[sl], krbuf.at[sl], sems.at[1, sl]).wait()

    @pl.when(s_idx == 0)
    def _():
        start_pages(0, 0, range(n_pages))

    wait_pages(slot)
    start_pages(jnp.minimum(s_idx + 1, n_seq - 1), 1 - slot, range(n_pages))

    seq = q_ref.shape[0]
    k = jnp.concatenate([kn_ref[...], krkr_ref[...]], axis=1)
    vt = v_ref[...].T
    row = lax.broadcasted_iota(jnp.int32, (Q_TILE, Q_TILE), 0)
    col = lax.broadcasted_iota(jnp.int32, (Q_TILE, Q_TILE), 1)
    nt = (((1,), (1,)), ((), ()))
    n_q = seq // Q_TILE
    n_chunks = n_pages * page_size // KEY_CHUNK

    def prompt_scores(qi):
        kend = (qi + 1) * Q_TILE
        q = q_ref[qi * Q_TILE:kend, :]
        s = lax.dot_general(k[:kend], q, nt, preferred_element_type=F32)
        diag = jnp.where(row <= col, s[kend - Q_TILE:, :], -jnp.inf)
        return diag if qi == 0 else jnp.concatenate([s[:kend - Q_TILE, :], diag], axis=0)

    def prompt_finish(qi, s):
        kend = (qi + 1) * Q_TILE
        p = jnp.exp(s - jnp.max(s, axis=0, keepdims=True))
        denom = jnp.sum(p, axis=0, keepdims=True)
        ot = jnp.dot(vt[:, :kend], p.astype(BF16), preferred_element_type=F32)
        o_ref[qi * Q_TILE:kend, :] = (ot / denom).T.astype(BF16)

    q_lat = qlat_ref[0]
    q_rope = qrope_ref[0]
    c_new = cnew_ref[0]
    kr_new = krnew_ref[0]

    def sample_scores(j):
        keys = slice(j * KEY_CHUNK, (j + 1) * KEY_CHUNK)
        cb = cbuf[slot, keys, :].astype(BF16)
        krt = krbuf[slot, :, keys].astype(BF16)
        s = (lax.dot_general(q_lat, cb, nt, preferred_element_type=F32)
             + jnp.dot(q_rope, krt, preferred_element_type=F32))
        return s, cb

    def sample_update(state, s, cb):
        m, denom, acc = state
        m_new = jnp.maximum(m, jnp.max(s, axis=1, keepdims=True))
        alpha = jnp.exp(m - m_new)
        p = jnp.exp(s - m_new)
        denom = denom * alpha + jnp.sum(p, axis=1, keepdims=True)
        acc = acc * alpha + jnp.dot(p.astype(BF16), cb, preferred_element_type=F32)
        return m_new, denom, acc

    m0 = (jnp.sum(q_lat.astype(F32) * c_new, axis=1, keepdims=True)
          + jnp.sum(q_rope.astype(F32) * kr_new, axis=1, keepdims=True))
    state = (m0, jnp.ones_like(m0), jnp.broadcast_to(c_new, (N_HEADS, KV_LORA_RANK)))

    per = n_q // n_chunks
    pending = [prompt_scores(i) for i in range(SCORE_LOOKAHEAD)]
    sc_next = sample_scores(0)
    for qi in range(n_q):
        s_cur = pending.pop(0)
        if qi + SCORE_LOOKAHEAD < n_q:
            pending.append(prompt_scores(qi + SCORE_LOOKAHEAD))
        prompt_finish(qi, s_cur)
        if qi % per == per - 1:
            j = qi // per
            sc_cur = sc_next
            if j + 1 < n_chunks:
                sc_next = sample_scores(j + 1)
            state = sample_update(state, *sc_cur)
    _, denom, acc = state
    olat_ref[0] = (acc / denom).astype(BF16)

    @pl.when(s_idx == n_seq - 1)
    def _():
        wait_pages(1 - slot)


def _attention(page_table, q_all, kn, krkr, v, q_lat, q_rope, c_new, kr_new, cache_c, cache_krt,
               batch, seq):
    n_seq, n_pages = page_table.shape
    page_size = cache_c.shape[1]
    past = n_pages * page_size
    assert n_seq == batch * N_HEADS and past % KEY_CHUNK == 0 and seq % Q_TILE == 0
    assert (seq // Q_TILE) % (past // KEY_CHUNK) == 0

    def per_seq(width):
        return pl.BlockSpec((1,) + width, lambda b, h, pt: (b * N_HEADS + h, 0, 0))

    grid_spec = pltpu.PrefetchScalarGridSpec(
        num_scalar_prefetch=1,
        grid=(batch, N_HEADS),
        in_specs=[pl.BlockSpec((seq, HEAD_DIM_PAD), lambda b, h, pt: (b, h)),
                  pl.BlockSpec((seq, QK_NOPE_DIM), lambda b, h, pt: (b, h)),
                  pl.BlockSpec((seq, LANES), lambda b, h, pt: (b, 0)),
                  pl.BlockSpec((seq, V_HEAD_DIM), lambda b, h, pt: (b, h)),
                  per_seq((N_HEADS, KV_LORA_RANK)), per_seq((N_HEADS, QK_ROPE_DIM)),
                  per_seq((1, KV_LORA_RANK)), per_seq((1, QK_ROPE_DIM)),
                  pl.BlockSpec(memory_space=pl.ANY), pl.BlockSpec(memory_space=pl.ANY)],
        out_specs=[pl.BlockSpec((seq, V_HEAD_DIM), lambda b, h, pt: (b, h)),
                   per_seq((N_HEADS, KV_LORA_RANK))],
        scratch_shapes=[pltpu.VMEM((2, past, KV_LORA_RANK), F32),
                        pltpu.VMEM((2, QK_ROPE_DIM, past), F32),
                        pltpu.SemaphoreType.DMA((2, 2))],
    )
    return pl.pallas_call(
        functools.partial(_attn_kernel, n_pages=n_pages, page_size=page_size),
        grid_spec=grid_spec,
        out_shape=[jax.ShapeDtypeStruct((batch * seq, N_HEADS * V_HEAD_DIM), BF16),
                   jax.ShapeDtypeStruct((n_seq, N_HEADS, KV_LORA_RANK), BF16)],
        compiler_params=_params(2),
        name="attention",
    )(page_table, q_all, kn, krkr, v, q_lat, q_rope, c_new, kr_new, cache_c, cache_krt)


def _rope_table(pos):
    inv = ROPE_THETA ** (-jnp.arange(0, QK_ROPE_DIM, 2, dtype=F32) / QK_ROPE_DIM)
    ang = pos[:, None] * inv[None, :]
    cos, sin = jnp.cos(ang), jnp.sin(ang)
    reps = LANES // QK_ROPE_DIM
    return jnp.concatenate([cos, cos] * reps + [-sin, sin] * reps, axis=1)


def kernel(x_prompt, x_sample, cache_kv_latent, cache_k_rope, page_table, norm_mix, sgu_w_in,
           sgu_v_norm, sgu_w_s, sgu_b_s, sgu_w_out, norm_ffn, ffn_w_in, ffn_w_out, kv_norm, w_dkv,
           kv_latent_norm, w_uk, w_uv, w_dq, q_norm, w_uq, w_o, final_norm):
    batch, seq, _ = x_prompt.shape
    n_sample, dec_seq, _ = x_sample.shape
    n_pages = page_table.shape[1]
    page_size = cache_kv_latent.shape[1]
    past_len = n_pages * page_size
    assert dec_seq == 1 and past_len % CHUNK == 0 and seq % TOKEN_TILE == 0
    assert n_sample == CHUNK
    n_a = sgu_w_in.shape[0]
    n_b = w_dq.shape[0]
    t_prompt = batch * seq

    bf = lambda a: a.astype(BF16)
    row = lambda a: a.reshape(1, -1)
    w_dkv_ext = bf(jnp.concatenate([w_dkv, w_dkv[:, KV_LORA_RANK:]], axis=1))
    w_uk2 = bf(w_uk.reshape(KV_LORA_RANK, N_HEADS * QK_NOPE_DIM))
    w_uv2 = bf(w_uv.reshape(KV_LORA_RANK, N_HEADS * V_HEAD_DIM))
    w_ukt = bf(jnp.transpose(w_uk, (1, 2, 0)))
    w_uvt = bf(jnp.transpose(w_uv, (1, 0, 2)))
    w_uq3 = w_uq.reshape(n_b, Q_LORA_RANK, N_HEADS, QK_NOPE_DIM + QK_ROPE_DIM)
    w_uq_ext = bf(jnp.concatenate(
        [w_uq3[..., :QK_NOPE_DIM].reshape(n_b, Q_LORA_RANK, -1),
         w_uq3[..., QK_NOPE_DIM:].reshape(n_b, Q_LORA_RANK, -1)], axis=2))
    tab_p = _rope_table(jnp.arange(seq, dtype=F32))
    tab_s = jnp.broadcast_to(_rope_table(jnp.full((1,), past_len, F32)), (n_sample, 2 * LANES))
    cache_krt = jnp.swapaxes(cache_k_rope, 1, 2)

    rows = lambda a: a.reshape(a.shape[0], 1, a.shape[1])
    norm_mix, norm_ffn, sgu_v_norm, q_norm = map(rows, (norm_mix, norm_ffn, sgu_v_norm, q_norm))
    sgu_w_in, sgu_w_out, ffn_w_in, ffn_w_out, w_dq, w_o = map(
        bf, (sgu_w_in, sgu_w_out, ffn_w_in, ffn_w_out, w_dq, w_o))
    sgu_b_t = jnp.swapaxes(sgu_b_s, 1, 2)

    hp = x_prompt.reshape(t_prompt, D_MODEL)
    hs = x_sample.reshape(n_sample, D_MODEL)
    v_rows = []
    for l in range(n_a):
        hp, hs, v_s = _sgu_layer(hp, hs, (norm_mix, l), (sgu_w_in, l), (sgu_v_norm, l),
                                 (sgu_w_s, l), (sgu_b_t, l), (sgu_w_out, l))
        v_rows.append(v_s)
        hp, hs = _ffn_layer(hp, hs, (norm_ffn, l), (ffn_w_in, l), (ffn_w_out, l))

    c_p, kr_p, kn, krkr, v, c_s, kr_s = _kv_layer(hp, hs, tab_p, tab_s, row(kv_norm), w_dkv_ext,
                                                  row(kv_latent_norm), w_uk2, w_uv2)
    c_new = c_s.reshape(n_sample, 1, KV_LORA_RANK)
    kr_new = kr_s.reshape(n_sample, 1, QK_ROPE_DIM)

    for j in range(n_b):
        l = n_a + j
        q_all, q_lat, q_rope = _q_layer(hp, hs, tab_p, tab_s, (norm_mix, l), (w_dq, j),
                                        (q_norm, j), (w_uq_ext, j), w_ukt)
        o_p, o_lat = _attention(page_table, q_all, kn, krkr, v,
                                q_lat.reshape(n_sample, N_HEADS, KV_LORA_RANK),
                                q_rope.reshape(n_sample, N_HEADS, QK_ROPE_DIM), c_new, kr_new,
                                cache_kv_latent, cache_krt, batch, seq)
        hp, hs = _ffn_layer(hp, hs, (norm_ffn, l), (ffn_w_in, l), (ffn_w_out, l),
                            attn=(o_p, o_lat.reshape(n_sample, N_HEADS * KV_LORA_RANK), w_uvt,
                                  (w_o, j)),
                            final=row(final_norm) if j == n_b - 1 else None)

    return (hp.reshape(batch, seq, D_MODEL),
            hs.reshape(n_sample, 1, D_MODEL),
            c_p.reshape(batch, seq, KV_LORA_RANK),
            kr_p.reshape(batch, seq, QK_ROPE_DIM),
            c_new,
            kr_new,
            jnp.stack(v_rows).reshape(n_a, n_sample, 1, D_SGU))
```

```python
import functools

import jax
import jax.numpy as jnp
from jax import lax
from jax.experimental import pallas as pl
from jax.experimental.pallas import tpu as pltpu

D_MODEL = 1024
CHUNK = 128
D_SGU = 2 * D_MODEL
SGU_GROUPS = 8
SGU_GROUP_DIM = D_SGU // SGU_GROUPS
N_HEADS = 16
QK_NOPE_DIM = 128
QK_ROPE_DIM = 64
V_HEAD_DIM = 128
Q_LORA_RANK = 384
KV_LORA_RANK = 256
ROPE_THETA = 10000.0
ATTN_SCALE = (QK_NOPE_DIM + QK_ROPE_DIM) ** -0.5
EPS = 1e-6

LANES = 128
TOKEN_TILE = 512
SGU_TOKEN_TILE = 256
FFN_COLS = 2 * LANES
HEAD_DIM_PAD = 2 * LANES
Q_TILE = 256
SCORE_LOOKAHEAD = 2
KEY_CHUNK = 1024
VMEM_LIMIT = 56 * 1024 * 1024

F32 = jnp.float32
BF16 = jnp.bfloat16


def _rms(x, g):
    return x * lax.rsqrt(jnp.mean(x * x, axis=-1, keepdims=True) + EPS) * g


def _rope_pairs(x, tab_ref):
    half = QK_ROPE_DIM // 2
    lane = lax.broadcasted_iota(jnp.int32, (1, LANES), 1)
    partner = jnp.where(lane % QK_ROPE_DIM < half,
                        pltpu.roll(x, LANES - half, 1), pltpu.roll(x, half, 1))
    return x * tab_ref[:, :LANES] + partner * tab_ref[:, LANES:]


def _const_spec(const):
    if isinstance(const, tuple):
        stack, layer = const
        zeros = (0,) * (stack.ndim - 1)
        return pl.BlockSpec((None,) + stack.shape[1:], lambda *_: (layer,) + zeros,
                            pipeline_mode=pl.Buffered(1))
    zeros = (0,) * const.ndim
    return pl.BlockSpec(const.shape, lambda *_: zeros, pipeline_mode=pl.Buffered(1))


def _params(n=1):
    return pltpu.CompilerParams(dimension_semantics=("arbitrary",) * n,
                                vmem_limit_bytes=VMEM_LIMIT)


def _two_stream_call(body, prompt_in, sample_in, consts, prompt_out, sample_out, *, name,
                     scratch=(), setup=None, tile=TOKEN_TILE):
    t_prompt = prompt_in[0].shape[0]
    n_sample = sample_in[0].shape[0]
    n_tiles = t_prompt // tile
    assert t_prompt % tile == 0
    n_in, n_c = len(prompt_in), len(consts)
    prompt_out = [po for po in prompt_out if po is not None]
    sample_out = [so for so in sample_out if so is not None]
    n_po, n_so = len(prompt_out), len(sample_out)

    def kern(*refs):
        refs = list(refs)
        p_in, refs = refs[:n_in], refs[n_in:]
        s_in, refs = refs[:n_in], refs[n_in:]
        c_in, refs = refs[:n_c], refs[n_c:]
        p_out, refs = refs[:n_po], refs[n_po:]
        s_out, scr = refs[:n_so], refs[n_so:]
        i = pl.program_id(0)
        if setup is not None:
            pl.when(i == 0)(lambda: setup(c_in, scr))
        pl.when(i < n_tiles)(lambda: body(p_in, c_in, p_out, scr, False))
        pl.when(i == n_tiles)(lambda: body(s_in, c_in, s_out, scr, True))

    def p_spec(width, rows=None):
        if rows is None:
            return pl.BlockSpec((tile, width), lambda i: (jnp.minimum(i, n_tiles - 1), 0))
        period = rows // tile
        return pl.BlockSpec((tile, width),
                            lambda i: (jnp.minimum(i, n_tiles - 1) % period, 0))

    def s_spec(rows, width):
        return pl.BlockSpec((rows, width), lambda i: (0, 0))

    in_specs = ([p_spec(a.shape[1], None if a.shape[0] == t_prompt else a.shape[0])
                 for a in prompt_in]
                + [s_spec(*a.shape) for a in sample_in]
                + [_const_spec(c) for c in consts])
    out_specs = ([p_spec(w) for w, _ in prompt_out] + [s_spec(n_sample, w) for w, _ in sample_out])
    out_shape = ([jax.ShapeDtypeStruct((t_prompt, w), d) for w, d in prompt_out]
                 + [jax.ShapeDtypeStruct((n_sample, w), d) for w, d in sample_out])
    return pl.pallas_call(
        kern, grid=(n_tiles + 1,), in_specs=in_specs, out_specs=out_specs, out_shape=out_shape,
        scratch_shapes=list(scratch), compiler_params=_params(), name=name,
    )(*prompt_in, *sample_in, *[c[0] if isinstance(c, tuple) else c for c in consts])


def _sgu_setup(consts, scratch):
    ws_ref, bt_ref = consts[3], consts[4]
    wmix_ref, bmix_ref = scratch
    r = lax.broadcasted_iota(jnp.int32, (CHUNK, CHUNK), 0)
    c = lax.broadcasted_iota(jnp.int32, (CHUNK, CHUNK), 1)
    for g in range(SGU_GROUPS):
        w = ws_ref[g]
        wmix_ref[0, g] = jnp.where(c <= r, w, 0.0).astype(BF16)
        wmix_ref[1, g] = jnp.where(c == r, w[0:1, 0:1], 0.0).astype(BF16)
    bmix_ref[0] = bt_ref[...]
    bmix_ref[1] = jnp.broadcast_to(bt_ref[0:1, :], bt_ref.shape)


def _sgu_body(ins, consts, outs, scratch, is_sample):
    (h_ref,) = ins
    g_ref, win_ref, vn_ref, _, _, wout_ref = consts
    wmix_ref, bmix_ref = scratch
    sel = 1 if is_sample else 0
    x = h_ref[...]
    hn = _rms(x, g_ref[...]).astype(BF16)
    z = jax.nn.gelu(jnp.dot(hn, win_ref[...], preferred_element_type=F32))
    u = z[:, :D_SGU]
    v = _rms(z[:, D_SGU:], vn_ref[...])
    if is_sample:
        outs[1][...] = v
    vb = v.astype(BF16)
    bias = bmix_ref[sel]
    ys = []
    for c in range(x.shape[0] // CHUNK):
        rows = slice(c * CHUNK, (c + 1) * CHUNK)
        parts = []
        for g in range(SGU_GROUPS):
            cols = slice(g * SGU_GROUP_DIM, (g + 1) * SGU_GROUP_DIM)
            m = jnp.dot(wmix_ref[sel, g], vb[rows, cols], preferred_element_type=F32)
            parts.append(m + bias[:, g:g + 1])
        ys.append((u[rows] * jnp.concatenate(parts, axis=1)).astype(BF16))
    y = ys[0] if len(ys) == 1 else jnp.concatenate(ys, axis=0)
    outs[0][...] = x + jnp.dot(y, wout_ref[...], preferred_element_type=F32)


def _sgu_layer(hp, hs, g, w_in, v_norm, w_s, b_t, w_out):
    return _two_stream_call(
        _sgu_body, [hp], [hs], [g, w_in, v_norm, w_s, b_t, w_out],
        [(D_MODEL, F32), None], [(D_MODEL, F32), (D_SGU, F32)], name="sgu_mixer", setup=_sgu_setup,
        tile=SGU_TOKEN_TILE,
        scratch=[pltpu.VMEM((2, SGU_GROUPS, CHUNK, CHUNK), BF16),
                 pltpu.VMEM((2, CHUNK, SGU_GROUPS), F32)])


def _ffn_body(ins, consts, outs, scratch, is_sample, *, d_ff, n_split, has_attn, has_final):
    del scratch
    consts = list(consts)
    x = ins[0][...]
    if has_attn:
        wuv_ref, wo_ref = consts.pop(0), consts.pop(0)
        if is_sample:
            o = jnp.concatenate(
                [jnp.dot(ins[1][:, hd * KV_LORA_RANK:(hd + 1) * KV_LORA_RANK], wuv_ref[hd],
                         preferred_element_type=F32).astype(BF16) for hd in range(N_HEADS)], axis=1)
        else:
            o = ins[1][...]
        x = x + jnp.dot(o, wo_ref[...], preferred_element_type=F32)
    g_ref, win_ref, wout_ref = consts[:3]
    hn = _rms(x, g_ref[...]).astype(BF16)
    width = d_ff // n_split

    def gate_up(j):
        return (jnp.dot(hn, win_ref[:, j * width:(j + 1) * width], preferred_element_type=F32),
                jnp.dot(hn, win_ref[:, d_ff + j * width:d_ff + (j + 1) * width],
                        preferred_element_type=F32))

    acts = []
    nxt = gate_up(0)
    for j in range(n_split):
        gate, up = nxt
        if j + 1 < n_split:
            nxt = gate_up(j + 1)
        acts.append((jax.nn.silu(gate) * up).astype(BF16))
    acc = x + jnp.dot(jnp.concatenate(acts, axis=1), wout_ref[...], preferred_element_type=F32)
    if has_final:
        acc = _rms(acc, consts[3][...])
    outs[0][...] = acc


def _ffn_layer(hp, hs, g, w_in, w_out, attn=None, final=None):
    prompt_in, sample_in, consts = [hp], [hs], []
    if attn is not None:
        o_prompt, o_lat_sample, w_uvt, w_o = attn
        prompt_in.append(o_prompt)
        sample_in.append(o_lat_sample)
        consts += [w_uvt, w_o]
    consts += [g, w_in, w_out]
    if final is not None:
        consts.append(final)
    d_ff = w_out[0].shape[-2]
    assert d_ff % FFN_COLS == 0
    body = functools.partial(_ffn_body, d_ff=d_ff, n_split=d_ff // FFN_COLS,
                             has_attn=attn is not None, has_final=final is not None)
    return _two_stream_call(body, prompt_in, sample_in, consts, [(D_MODEL, F32)], [(D_MODEL, F32)],
                            name="ffn")


def _kv_body(ins, consts, outs, scratch, is_sample):
    del scratch
    h_ref, tab_ref = ins
    g_ref, wdkv_ref, ln_ref, wuk_ref, wuv_ref = consts
    hn = _rms(h_ref[...], g_ref[...]).astype(BF16)
    ckr = jnp.dot(hn, wdkv_ref[...], preferred_element_type=F32)
    c = _rms(ckr[:, :KV_LORA_RANK], ln_ref[...])
    krkr = _rope_pairs(ckr[:, KV_LORA_RANK:], tab_ref)
    outs[0][...] = c
    outs[1][...] = krkr[:, :QK_ROPE_DIM]
    if not is_sample:
        cb = c.astype(BF16)
        outs[2][...] = jnp.dot(cb, wuk_ref[...], preferred_element_type=F32).astype(BF16)
        outs[3][...] = krkr.astype(BF16)
        outs[4][...] = jnp.dot(cb, wuv_ref[...], preferred_element_type=F32).astype(BF16)


def _kv_layer(hp, hs, tab_p, tab_s, g, w_dkv_ext, lat_norm, w_uk2, w_uv2):
    return _two_stream_call(
        _kv_body, [hp, tab_p], [hs, tab_s], [g, w_dkv_ext, lat_norm, w_uk2, w_uv2],
        [(KV_LORA_RANK, F32), (QK_ROPE_DIM, F32), (N_HEADS * QK_NOPE_DIM, BF16), (LANES, BF16),
         (N_HEADS * V_HEAD_DIM, BF16)],
        [(KV_LORA_RANK, F32), (QK_ROPE_DIM, F32), None, None, None], name="shared_kv")


def _q_body(ins, consts, outs, scratch, is_sample):
    del scratch
    h_ref, tab_ref = ins
    g_ref, wdq_ref, qn_ref, wuq_ref, wukt_ref = consts
    hn = _rms(h_ref[...], g_ref[...]).astype(BF16)
    cq = _rms(jnp.dot(hn, wdq_ref[...], preferred_element_type=F32), qn_ref[...]).astype(BF16)
    z = jnp.dot(cq, wuq_ref[...], preferred_element_type=F32) * ATTN_SCALE
    nope_w = N_HEADS * QK_NOPE_DIM
    low = lax.broadcasted_iota(jnp.int32, (1, LANES), 1) < QK_ROPE_DIM
    for t in range(N_HEADS // 2):
        rt = _rope_pairs(z[:, nope_w + t * LANES:nope_w + (t + 1) * LANES], tab_ref)
        if is_sample:
            outs[1][:, t * LANES:(t + 1) * LANES] = rt.astype(BF16)
            continue
        for half in range(2):
            hd = 2 * t + half
            base = hd * HEAD_DIM_PAD
            outs[0][:, base:base + LANES] = z[:, hd * LANES:(hd + 1) * LANES].astype(BF16)
            keep = low if half == 0 else jnp.logical_not(low)
            outs[0][:, base + LANES:base + 2 * LANES] = jnp.where(keep, rt, 0.0).astype(BF16)
    if is_sample:
        for hd in range(N_HEADS):
            qn = z[:, hd * LANES:(hd + 1) * LANES].astype(BF16)
            outs[0][:, hd * KV_LORA_RANK:(hd + 1) * KV_LORA_RANK] = jnp.dot(
                qn, wukt_ref[hd], preferred_element_type=F32).astype(BF16)


def _q_layer(hp, hs, tab_p, tab_s, g, w_dq, q_norm, w_uq_ext, w_ukt):
    return _two_stream_call(
        _q_body, [hp, tab_p], [hs, tab_s], [g, w_dq, q_norm, w_uq_ext, w_ukt],
        [(N_HEADS * HEAD_DIM_PAD, BF16), None],
        [(N_HEADS * KV_LORA_RANK, BF16), (N_HEADS * QK_ROPE_DIM, BF16)], name="mla_query")


def _attn_kernel(pt_ref, q_ref, kn_ref, krkr_ref, v_ref, qlat_ref, qrope_ref, cnew_ref, krnew_ref,
                 cc_ref, ckr_ref, o_ref, olat_ref, cbuf, krbuf, sems, *, n_pages, page_size):
    n_seq = pl.num_programs(0) * pl.num_programs(1)
    s_idx = pl.program_id(0) * pl.num_programs(1) + pl.program_id(1)
    slot = s_idx % 2

    def start_pages(seq, sl, pages):
        for p in pages:
            page = pt_ref[seq, p]
            keys = pl.ds(p * page_size, page_size)
            pltpu.make_async_copy(cc_ref.at[page], cbuf.at[sl, keys], sems.at[0, sl]).start()
            pltpu.make_async_copy(ckr_ref.at[page], krbuf.at[sl, :, keys], sems.at[1, sl]).start()

    def wait_pages(sl):
        pltpu.make_async_copy(cbuf.at[sl], cbuf.at[sl], sems.at[0, sl]).wait()
        pltpu.make_async_copy(krbuf.at[sl], krbuf.at[sl], sems.at[1, sl]).wait()

    @pl.when(s_idx == 0)
    def _():
        start_pages(0, 0, range(n_pages))

    wait_pages(slot)
    start_pages(jnp.minimum(s_idx + 1, n_seq - 1), 1 - slot, range(n_pages))

    seq = q_ref.shape[0]
    k = jnp.concatenate([kn_ref[...], krkr_ref[...]], axis=1)
    vt = v_ref[...].T
    row = lax.broadcasted_iota(jnp.int32, (Q_TILE, Q_TILE), 0)
    col = lax.broadcasted_iota(jnp.int32, (Q_TILE, Q_TILE), 1)
    nt = (((1,), (1,)), ((), ()))
    n_q = seq // Q_TILE
    n_chunks = n_pages * page_size // KEY_CHUNK

    def prompt_scores(qi):
        kend = (qi + 1) * Q_TILE
        q = q_ref[qi * Q_TILE:kend, :]
        s = lax.dot_general(k[:kend], q, nt, preferred_element_type=F32)
        diag = jnp.where(row <= col, s[kend - Q_TILE:, :], -jnp.inf)
        return diag if qi == 0 else jnp.concatenate([s[:kend - Q_TILE, :], diag], axis=0)

    def prompt_finish(qi, s):
        kend = (qi + 1) * Q_TILE
        p = jnp.exp(s - jnp.max(s, axis=0, keepdims=True))
        denom = jnp.sum(p, axis=0, keepdims=True)
        ot = jnp.dot(vt[:, :kend], p.astype(BF16), preferred_element_type=F32)
        o_ref[qi * Q_TILE:kend, :] = (ot / denom).T.astype(BF16)

    q_lat = qlat_ref[0]
    q_rope = qrope_ref[0]
    c_new = cnew_ref[0]
    kr_new = krnew_ref[0]

    def sample_scores(j):
        keys = slice(j * KEY_CHUNK, (j + 1) * KEY_CHUNK)
        cb = cbuf[slot, keys, :].astype(BF16)
        krt = krbuf[slot, :, keys].astype(BF16)
        s = (lax.dot_general(q_lat, cb, nt, preferred_element_type=F32)
             + jnp.dot(q_rope, krt, preferred_element_type=F32))
        return s, cb

    def sample_update(state, s, cb):
        m, denom, acc = state
        m_new = jnp.maximum(m, jnp.max(s, axis=1, keepdims=True))
        alpha = jnp.exp(m - m_new)
        p = jnp.exp(s - m_new)
        denom = denom * alpha + jnp.sum(p, axis=1, keepdims=True)
        acc = acc * alpha + jnp.dot(p.astype(BF16), cb, preferred_element_type=F32)
        return m_new, denom, acc

    m0 = (jnp.sum(q_lat.astype(F32) * c_new, axis=1, keepdims=True)
          + jnp.sum(q_rope.astype(F32) * kr_new, axis=1, keepdims=True))
    state = (m0, jnp.ones_like(m0), jnp.broadcast_to(c_new, (N_HEADS, KV_LORA_RANK)))

    per = n_q // n_chunks
    pending = [prompt_scores(i) for i in range(SCORE_LOOKAHEAD)]
    sc_next = sample_scores(0)
    for qi in range(n_q):
        s_cur = pending.pop(0)
        if qi + SCORE_LOOKAHEAD < n_q:
            pending.append(prompt_scores(qi + SCORE_LOOKAHEAD))
        prompt_finish(qi, s_cur)
        if qi % per == per - 1:
            j = qi // per
            sc_cur = sc_next
            if j + 1 < n_chunks:
                sc_next = sample_scores(j + 1)
            state = sample_update(state, *sc_cur)
    _, denom, acc = state
    olat_ref[0] = (acc / denom).astype(BF16)

    @pl.when(s_idx == n_seq - 1)
    def _():
        wait_pages(1 - slot)


def _attention(page_table, q_all, kn, krkr, v, q_lat, q_rope, c_new, kr_new, cache_c, cache_krt,
               batch, seq):
    n_seq, n_pages = page_table.shape
    page_size = cache_c.shape[1]
    past = n_pages * page_size
    assert n_seq == batch * N_HEADS and past % KEY_CHUNK == 0 and seq % Q_TILE == 0
    assert (seq // Q_TILE) % (past // KEY_CHUNK) == 0

    def per_seq(width):
        return pl.BlockSpec((1,) + width, lambda b, h, pt: (b * N_HEADS + h, 0, 0))

    grid_spec = pltpu.PrefetchScalarGridSpec(
        num_scalar_prefetch=1,
        grid=(batch, N_HEADS),
        in_specs=[pl.BlockSpec((seq, HEAD_DIM_PAD), lambda b, h, pt: (b, h)),
                  pl.BlockSpec((seq, QK_NOPE_DIM), lambda b, h, pt: (b, h)),
                  pl.BlockSpec((seq, LANES), lambda b, h, pt: (b, 0)),
                  pl.BlockSpec((seq, V_HEAD_DIM), lambda b, h, pt: (b, h)),
                  per_seq((N_HEADS, KV_LORA_RANK)), per_seq((N_HEADS, QK_ROPE_DIM)),
                  per_seq((1, KV_LORA_RANK)), per_seq((1, QK_ROPE_DIM)),
                  pl.BlockSpec(memory_space=pl.ANY), pl.BlockSpec(memory_space=pl.ANY)],
        out_specs=[pl.BlockSpec((seq, V_HEAD_DIM), lambda b, h, pt: (b, h)),
                   per_seq((N_HEADS, KV_LORA_RANK))],
        scratch_shapes=[pltpu.VMEM((2, past, KV_LORA_RANK), F32),
                        pltpu.VMEM((2, QK_ROPE_DIM, past), F32),
                        pltpu.SemaphoreType.DMA((2, 2))],
    )
    return pl.pallas_call(
        functools.partial(_attn_kernel, n_pages=n_pages, page_size=page_size),
        grid_spec=grid_spec,
        out_shape=[jax.ShapeDtypeStruct((batch * seq, N_HEADS * V_HEAD_DIM), BF16),
                   jax.ShapeDtypeStruct((n_seq, N_HEADS, KV_LORA_RANK), BF16)],
        compiler_params=_params(2),
        name="attention",
    )(page_table, q_all, kn, krkr, v, q_lat, q_rope, c_new, kr_new, cache_c, cache_krt)


def _rope_table(pos):
    inv = ROPE_THETA ** (-jnp.arange(0, QK_ROPE_DIM, 2, dtype=F32) / QK_ROPE_DIM)
    ang = pos[:, None] * inv[None, :]
    cos, sin = jnp.cos(ang), jnp.sin(ang)
    reps = LANES // QK_ROPE_DIM
    return jnp.concatenate([cos, cos] * reps + [-sin, sin] * reps, axis=1)


def kernel(x_prompt, x_sample, cache_kv_latent, cache_k_rope, page_table, norm_mix, sgu_w_in,
           sgu_v_norm, sgu_w_s, sgu_b_s, sgu_w_out, norm_ffn, ffn_w_in, ffn_w_out, kv_norm, w_dkv,
           kv_latent_norm, w_uk, w_uv, w_dq, q_norm, w_uq, w_o, final_norm):
    batch, seq, _ = x_prompt.shape
    n_sample, dec_seq, _ = x_sample.shape
    n_pages = page_table.shape[1]
    page_size = cache_kv_latent.shape[1]
    past_len = n_pages * page_size
    assert dec_seq == 1 and past_len % CHUNK == 0 and seq % TOKEN_TILE == 0
    assert n_sample == CHUNK
    n_a = sgu_w_in.shape[0]
    n_b = w_dq.shape[0]
    t_prompt = batch * seq

    bf = lambda a: a.astype(BF16)
    row = lambda a: a.reshape(1, -1)
    w_dkv_ext = bf(jnp.concatenate([w_dkv, w_dkv[:, KV_LORA_RANK:]], axis=1))
    w_uk2 = bf(w_uk.reshape(KV_LORA_RANK, N_HEADS * QK_NOPE_DIM))
    w_uv2 = bf(w_uv.reshape(KV_LORA_RANK, N_HEADS * V_HEAD_DIM))
    w_ukt = bf(jnp.transpose(w_uk, (1, 2, 0)))
    w_uvt = bf(jnp.transpose(w_uv, (1, 0, 2)))
    w_uq3 = w_uq.reshape(n_b, Q_LORA_RANK, N_HEADS, QK_NOPE_DIM + QK_ROPE_DIM)
    w_uq_ext = bf(jnp.concatenate(
        [w_uq3[..., :QK_NOPE_DIM].reshape(n_b, Q_LORA_RANK, -1),
         w_uq3[..., QK_NOPE_DIM:].reshape(n_b, Q_LORA_RANK, -1)], axis=2))
    tab_p = _rope_table(jnp.arange(seq, dtype=F32))
    tab_s = jnp.broadcast_to(_rope_table(jnp.full((1,), past_len, F32)), (n_sample, 2 * LANES))
    cache_krt = jnp.swapaxes(cache_k_rope, 1, 2)

    rows = lambda a: a.reshape(a.shape[0], 1, a.shape[1])
    norm_mix, norm_ffn, sgu_v_norm, q_norm = map(rows, (norm_mix, norm_ffn, sgu_v_norm, q_norm))
    sgu_w_in, sgu_w_out, ffn_w_in, ffn_w_out, w_dq, w_o = map(
        bf, (sgu_w_in, sgu_w_out, ffn_w_in, ffn_w_out, w_dq, w_o))
    sgu_b_t = jnp.swapaxes(sgu_b_s, 1, 2)

    hp = x_prompt.reshape(t_prompt, D_MODEL)
    hs = x_sample.reshape(n_sample, D_MODEL)
    v_rows = []
    for l in range(n_a):
        hp, hs, v_s = _sgu_layer(hp, hs, (norm_mix, l), (sgu_w_in, l), (sgu_v_norm, l),
                                 (sgu_w_s, l), (sgu_b_t, l), (sgu_w_out, l))
        v_rows.append(v_s)
        hp, hs = _ffn_layer(hp, hs, (norm_ffn, l), (ffn_w_in, l), (ffn_w_out, l))

    c_p, kr_p, kn, krkr, v, c_s, kr_s = _kv_layer(hp, hs, tab_p, tab_s, row(kv_norm), w_dkv_ext,
                                                  row(kv_latent_norm), w_uk2, w_uv2)
    c_new = c_s.reshape(n_sample, 1, KV_LORA_RANK)
    kr_new = kr_s.reshape(n_sample, 1, QK_ROPE_DIM)

    for j in range(n_b):
        l = n_a + j
        q_all, q_lat, q_rope = _q_layer(hp, hs, tab_p, tab_s, (norm_mix, l), (w_dq, j),
                                        (q_norm, j), (w_uq_ext, j), w_ukt)
        o_p, o_lat = _attention(page_table, q_all, kn, krkr, v,
                                q_lat.reshape(n_sample, N_HEADS, KV_LORA_RANK),
                                q_rope.reshape(n_sample, N_HEADS, QK_ROPE_DIM), c_new, kr_new,
                                cache_kv_latent, cache_krt, batch, seq)
        hp, hs = _ffn_layer(hp, hs, (norm_ffn, l), (ffn_w_in, l), (ffn_w_out, l),
                            attn=(o_p, o_lat.reshape(n_sample, N_HEADS * KV_LORA_RANK), w_uvt,
                                  (w_o, j)),
                            final=row(final_norm) if j == n_b - 1 else None)

    return (hp.reshape(batch, seq, D_MODEL),
            hs.reshape(n_sample, 1, D_MODEL),
            c_p.reshape(batch, seq, KV_LORA_RANK),
            kr_p.reshape(batch, seq, QK_ROPE_DIM),
            c_new,
            kr_new,
            jnp.stack(v_rows).reshape(n_a, n_sample, 1, D_SGU))
```

```python
import functools

import jax
import jax.numpy as jnp
from jax import lax
from jax.experimental import pallas as pl
from jax.experimental.pallas import tpu as pltpu

D_MODEL = 1024
CHUNK = 128
D_SGU = 2 * D_MODEL
SGU_GROUPS = 8
SGU_GROUP_DIM = D_SGU // SGU_GROUPS
N_HEADS = 16
QK_NOPE_DIM = 128
QK_ROPE_DIM = 64
V_HEAD_DIM = 128
Q_LORA_RANK = 384
KV_LORA_RANK = 256
ROPE_THETA = 10000.0
ATTN_SCALE = (QK_NOPE_DIM + QK_ROPE_DIM) ** -0.5
EPS = 1e-6

LANES = 128
TOKEN_TILE = 512
FFN_COLS = 2 * LANES
HEAD_DIM_PAD = 2 * LANES
Q_TILE = 256
SCORE_LOOKAHEAD = 2
KEY_CHUNK = 1024
VMEM_LIMIT = 56 * 1024 * 1024

F32 = jnp.float32
BF16 = jnp.bfloat16


def _rms(x, g):
    return x * lax.rsqrt(jnp.mean(x * x, axis=-1, keepdims=True) + EPS) * g


def _rope_pairs(x, tab_ref):
    half = QK_ROPE_DIM // 2
    lane = lax.broadcasted_iota(jnp.int32, (1, LANES), 1)
    partner = jnp.where(lane % QK_ROPE_DIM < half,
                        pltpu.roll(x, LANES - half, 1), pltpu.roll(x, half, 1))
    return x * tab_ref[:, :LANES] + partner * tab_ref[:, LANES:]


def _const_spec(const):
    if isinstance(const, tuple):
        stack, layer = const
        zeros = (0,) * (stack.ndim - 1)
        return pl.BlockSpec((None,) + stack.shape[1:], lambda *_: (layer,) + zeros,
                            pipeline_mode=pl.Buffered(1))
    zeros = (0,) * const.ndim
    return pl.BlockSpec(const.shape, lambda *_: zeros, pipeline_mode=pl.Buffered(1))


def _params(n=1):
    return pltpu.CompilerParams(dimension_semantics=("arbitrary",) * n,
                                vmem_limit_bytes=VMEM_LIMIT)


def _two_stream_call(body, prompt_in, sample_in, consts, prompt_out, sample_out, *, name,
                     scratch=(), setup=None):
    t_prompt = prompt_in[0].shape[0]
    n_sample = sample_in[0].shape[0]
    n_tiles = t_prompt // TOKEN_TILE
    assert t_prompt % TOKEN_TILE == 0
    n_in, n_c = len(prompt_in), len(consts)
    prompt_out = [po for po in prompt_out if po is not None]
    sample_out = [so for so in sample_out if so is not None]
    n_po, n_so = len(prompt_out), len(sample_out)

    def kern(*refs):
        refs = list(refs)
        p_in, refs = refs[:n_in], refs[n_in:]
        s_in, refs = refs[:n_in], refs[n_in:]
        c_in, refs = refs[:n_c], refs[n_c:]
        p_out, refs = refs[:n_po], refs[n_po:]
        s_out, scr = refs[:n_so], refs[n_so:]
        i = pl.program_id(0)
        if setup is not None:
            pl.when(i == 0)(lambda: setup(c_in, scr))
        pl.when(i < n_tiles)(lambda: body(p_in, c_in, p_out, scr, False))
        pl.when(i == n_tiles)(lambda: body(s_in, c_in, s_out, scr, True))

    def p_spec(width, rows=None):
        if rows is None:
            return pl.BlockSpec((TOKEN_TILE, width), lambda i: (jnp.minimum(i, n_tiles - 1), 0))
        period = rows // TOKEN_TILE
        return pl.BlockSpec((TOKEN_TILE, width),
                            lambda i: (jnp.minimum(i, n_tiles - 1) % period, 0))

    def s_spec(rows, width):
        return pl.BlockSpec((rows, width), lambda i: (0, 0))

    in_specs = ([p_spec(a.shape[1], None if a.shape[0] == t_prompt else a.shape[0])
                 for a in prompt_in]
                + [s_spec(*a.shape) for a in sample_in]
                + [_const_spec(c) for c in consts])
    out_specs = ([p_spec(w) for w, _ in prompt_out] + [s_spec(n_sample, w) for w, _ in sample_out])
    out_shape = ([jax.ShapeDtypeStruct((t_prompt, w), d) for w, d in prompt_out]
                 + [jax.ShapeDtypeStruct((n_sample, w), d) for w, d in sample_out])
    return pl.pallas_call(
        kern, grid=(n_tiles + 1,), in_specs=in_specs, out_specs=out_specs, out_shape=out_shape,
        scratch_shapes=list(scratch), compiler_params=_params(), name=name,
    )(*prompt_in, *sample_in, *[c[0] if isinstance(c, tuple) else c for c in consts])


def _sgu_setup(consts, scratch):
    ws_ref, bt_ref = consts[3], consts[4]
    wmix_ref, bmix_ref = scratch
    r = lax.broadcasted_iota(jnp.int32, (CHUNK, CHUNK), 0)
    c = lax.broadcasted_iota(jnp.int32, (CHUNK, CHUNK), 1)
    for g in range(SGU_GROUPS):
        w = ws_ref[g]
        wmix_ref[0, g] = jnp.where(c <= r, w, 0.0).astype(BF16)
        wmix_ref[1, g] = jnp.where(c == r, w[0:1, 0:1], 0.0).astype(BF16)
    bmix_ref[0] = bt_ref[...]
    bmix_ref[1] = jnp.broadcast_to(bt_ref[0:1, :], bt_ref.shape)


def _sgu_body(ins, consts, outs, scratch, is_sample):
    (h_ref,) = ins
    g_ref, win_ref, vn_ref, _, _, wout_ref = consts
    wmix_ref, bmix_ref = scratch
    sel = 1 if is_sample else 0
    x = h_ref[...]
    hn = _rms(x, g_ref[...]).astype(BF16)
    z = jax.nn.gelu(jnp.dot(hn, win_ref[...], preferred_element_type=F32))
    u = z[:, :D_SGU]
    v = _rms(z[:, D_SGU:], vn_ref[...])
    if is_sample:
        outs[1][...] = v
    vb = v.astype(BF16)
    bias = bmix_ref[sel]
    ys = []
    for c in range(x.shape[0] // CHUNK):
        rows = slice(c * CHUNK, (c + 1) * CHUNK)
        parts = []
        for g in range(SGU_GROUPS):
            cols = slice(g * SGU_GROUP_DIM, (g + 1) * SGU_GROUP_DIM)
            m = jnp.dot(wmix_ref[sel, g], vb[rows, cols], preferred_element_type=F32)
            parts.append(m + bias[:, g:g + 1])
        ys.append((u[rows] * jnp.concatenate(parts, axis=1)).astype(BF16))
    y = ys[0] if len(ys) == 1 else jnp.concatenate(ys, axis=0)
    outs[0][...] = x + jnp.dot(y, wout_ref[...], preferred_element_type=F32)


def _sgu_layer(hp, hs, g, w_in, v_norm, w_s, b_t, w_out):
    return _two_stream_call(
        _sgu_body, [hp], [hs], [g, w_in, v_norm, w_s, b_t, w_out],
        [(D_MODEL, F32), None], [(D_MODEL, F32), (D_SGU, F32)], name="sgu_mixer", setup=_sgu_setup,
        scratch=[pltpu.VMEM((2, SGU_GROUPS, CHUNK, CHUNK), BF16),
                 pltpu.VMEM((2, CHUNK, SGU_GROUPS), F32)])


def _ffn_body(ins, consts, outs, scratch, is_sample, *, d_ff, n_split, has_attn, has_final):
    del scratch
    consts = list(consts)
    x = ins[0][...]
    if has_attn:
        wuv_ref, wo_ref = consts.pop(0), consts.pop(0)
        if is_sample:
            o = jnp.concatenate(
                [jnp.dot(ins[1][:, hd * KV_LORA_RANK:(hd + 1) * KV_LORA_RANK], wuv_ref[hd],
                         preferred_element_type=F32).astype(BF16) for hd in range(N_HEADS)], axis=1)
        else:
            o = ins[1][...]
        x = x + jnp.dot(o, wo_ref[...], preferred_element_type=F32)
    g_ref, win_ref, wout_ref = consts[:3]
    hn = _rms(x, g_ref[...]).astype(BF16)
    width = d_ff // n_split

    def gate_up(j):
        return (jnp.dot(hn, win_ref[:, j * width:(j + 1) * width], preferred_element_type=F32),
                jnp.dot(hn, win_ref[:, d_ff + j * width:d_ff + (j + 1) * width],
                        preferred_element_type=F32))

    acts = []
    nxt = gate_up(0)
    for j in range(n_split):
        gate, up = nxt
        if j + 1 < n_split:
            nxt = gate_up(j + 1)
        acts.append((jax.nn.silu(gate) * up).astype(BF16))
    acc = x + jnp.dot(jnp.concatenate(acts, axis=1), wout_ref[...], preferred_element_type=F32)
    if has_final:
        acc = _rms(acc, consts[3][...])
    outs[0][...] = acc


def _ffn_layer(hp, hs, g, w_in, w_out, attn=None, final=None):
    prompt_in, sample_in, consts = [hp], [hs], []
    if attn is not None:
        o_prompt, o_lat_sample, w_uvt, w_o = attn
        prompt_in.append(o_prompt)
        sample_in.append(o_lat_sample)
        consts += [w_uvt, w_o]
    consts += [g, w_in, w_out]
    if final is not None:
        consts.append(final)
    d_ff = w_out[0].shape[-2]
    assert d_ff % FFN_COLS == 0
    body = functools.partial(_ffn_body, d_ff=d_ff, n_split=d_ff // FFN_COLS,
                             has_attn=attn is not None, has_final=final is not None)
    return _two_stream_call(body, prompt_in, sample_in, consts, [(D_MODEL, F32)], [(D_MODEL, F32)],
                            name="ffn")


def _kv_body(ins, consts, outs, scratch, is_sample):
    del scratch
    h_ref, tab_ref = ins
    g_ref, wdkv_ref, ln_ref, wuk_ref, wuv_ref = consts
    hn = _rms(h_ref[...], g_ref[...]).astype(BF16)
    ckr = jnp.dot(hn, wdkv_ref[...], preferred_element_type=F32)
    c = _rms(ckr[:, :KV_LORA_RANK], ln_ref[...])
    krkr = _rope_pairs(ckr[:, KV_LORA_RANK:], tab_ref)
    outs[0][...] = c
    outs[1][...] = krkr[:, :QK_ROPE_DIM]
    if not is_sample:
        cb = c.astype(BF16)
        outs[2][...] = jnp.dot(cb, wuk_ref[...], preferred_element_type=F32).astype(BF16)
        outs[3][...] = krkr.astype(BF16)
        outs[4][...] = jnp.dot(cb, wuv_ref[...], preferred_element_type=F32).astype(BF16)


def _kv_layer(hp, hs, tab_p, tab_s, g, w_dkv_ext, lat_norm, w_uk2, w_uv2):
    return _two_stream_call(
        _kv_body, [hp, tab_p], [hs, tab_s], [g, w_dkv_ext, lat_norm, w_uk2, w_uv2],
        [(KV_LORA_RANK, F32), (QK_ROPE_DIM, F32), (N_HEADS * QK_NOPE_DIM, BF16), (LANES, BF16),
         (N_HEADS * V_HEAD_DIM, BF16)],
        [(KV_LORA_RANK, F32), (QK_ROPE_DIM, F32), None, None, None], name="shared_kv")


def _q_body(ins, consts, outs, scratch, is_sample):
    del scratch
    h_ref, tab_ref = ins
    g_ref, wdq_ref, qn_ref, wuq_ref, wukt_ref = consts
    hn = _rms(h_ref[...], g_ref[...]).astype(BF16)
    cq = _rms(jnp.dot(hn, wdq_ref[...], preferred_element_type=F32), qn_ref[...]).astype(BF16)
    z = jnp.dot(cq, wuq_ref[...], preferred_element_type=F32) * ATTN_SCALE
    nope_w = N_HEADS * QK_NOPE_DIM
    low = lax.broadcasted_iota(jnp.int32, (1, LANES), 1) < QK_ROPE_DIM
    for t in range(N_HEADS // 2):
        rt = _rope_pairs(z[:, nope_w + t * LANES:nope_w + (t + 1) * LANES], tab_ref)
        if is_sample:
            outs[1][:, t * LANES:(t + 1) * LANES] = rt.astype(BF16)
            continue
        for half in range(2):
            hd = 2 * t + half
            base = hd * HEAD_DIM_PAD
            outs[0][:, base:base + LANES] = z[:, hd * LANES:(hd + 1) * LANES].astype(BF16)
            keep = low if half == 0 else jnp.logical_not(low)
            outs[0][:, base + LANES:base + 2 * LANES] = jnp.where(keep, rt, 0.0).astype(BF16)
    if is_sample:
        for hd in range(N_HEADS):
            qn = z[:, hd * LANES:(hd + 1) * LANES].astype(BF16)
            outs[0][:, hd * KV_LORA_RANK:(hd + 1) * KV_LORA_RANK] = jnp.dot(
                qn, wukt_ref[hd], preferred_element_type=F32).astype(BF16)


def _q_layer(hp, hs, tab_p, tab_s, g, w_dq, q_norm, w_uq_ext, w_ukt):
    return _two_stream_call(
        _q_body, [hp, tab_p], [hs, tab_s], [g, w_dq, q_norm, w_uq_ext, w_ukt],
        [(N_HEADS * HEAD_DIM_PAD, BF16), None],
        [(N_HEADS * KV_LORA_RANK, BF16), (N_HEADS * QK_ROPE_DIM, BF16)], name="mla_query")


def _attn_kernel(pt_ref, q_ref, kn_ref, krkr_ref, v_ref, qlat_ref, qrope_ref, cnew_ref, krnew_ref,
                 cc_ref, ckr_ref, o_ref, olat_ref, cbuf, krbuf, sems, *, n_pages, page_size):
    n_seq = pl.num_programs(0) * pl.num_programs(1)
    s_idx = pl.program_id(0) * pl.num_programs(1) + pl.program_id(1)
    slot = s_idx % 2

    def start_pages(seq, sl, pages):
        for p in pages:
            page = pt_ref[seq, p]
            keys = pl.ds(p * page_size, page_size)
            pltpu.make_async_copy(cc_ref.at[page], cbuf.at[sl, keys], sems.at[0, sl]).start(
                priority=p % 2)
            pltpu.make_async_copy(ckr_ref.at[page], krbuf.at[sl, :, keys], sems.at[1, sl]).start(
                priority=(p + 1) % 2)

    def wait_pages(sl):
        pltpu.make_async_copy(cbuf.at[sl], cbuf.at[sl], sems.at[0, sl]).wait()
        pltpu.make_async_copy(krbuf.at[sl], krbuf.at[sl], sems.at[1, sl]).wait()

    @pl.when(s_idx == 0)
    def _():
        start_pages(0, 0, range(n_pages))

    wait_pages(slot)
    start_pages(jnp.minimum(s_idx + 1, n_seq - 1), 1 - slot, range(n_pages))

    seq = q_ref.shape[0]
    k = jnp.concatenate([kn_ref[...], krkr_ref[...]], axis=1)
    vt = v_ref[...].T
    row = lax.broadcasted_iota(jnp.int32, (Q_TILE, Q_TILE), 0)
    col = lax.broadcasted_iota(jnp.int32, (Q_TILE, Q_TILE), 1)
    nt = (((1,), (1,)), ((), ()))
    n_q = seq // Q_TILE
    n_chunks = n_pages * page_size // KEY_CHUNK

    def prompt_scores(qi):
        kend = (qi + 1) * Q_TILE
        q = q_ref[qi * Q_TILE:kend, :]
        s = lax.dot_general(k[:kend], q, nt, preferred_element_type=F32)
        diag = jnp.where(row <= col, s[kend - Q_TILE:, :], -jnp.inf)
        return diag if qi == 0 else jnp.concatenate([s[:kend - Q_TILE, :], diag], axis=0)

    def prompt_finish(qi, s):
        kend = (qi + 1) * Q_TILE
        p = jnp.exp(s - jnp.max(s, axis=0, keepdims=True))
        denom = jnp.sum(p, axis=0, keepdims=True)
        ot = jnp.dot(vt[:, :kend], p.astype(BF16), preferred_element_type=F32)
        o_ref[qi * Q_TILE:kend, :] = (ot / denom).T.astype(BF16)

    q_lat = qlat_ref[0]
    q_rope = qrope_ref[0]
    c_new = cnew_ref[0]
    kr_new = krnew_ref[0]

    def sample_scores(j):
        keys = slice(j * KEY_CHUNK, (j + 1) * KEY_CHUNK)
        cb = cbuf[slot, keys, :].astype(BF16)
        krt = krbuf[slot, :, keys].astype(BF16)
        s = (lax.dot_general(q_lat, cb, nt, preferred_element_type=F32)
             + jnp.dot(q_rope, krt, preferred_element_type=F32))
        return s, cb

    def sample_update(state, s, cb):
        m, denom, acc = state
        m_new = jnp.maximum(m, jnp.max(s, axis=1, keepdims=True))
        alpha = jnp.exp(m - m_new)
        p = jnp.exp(s - m_new)
        denom = denom * alpha + jnp.sum(p, axis=1, keepdims=True)
        acc = acc * alpha + jnp.dot(p.astype(BF16), cb, preferred_element_type=F32)
        return m_new, denom, acc

    m0 = (jnp.sum(q_lat.astype(F32) * c_new, axis=1, keepdims=True)
          + jnp.sum(q_rope.astype(F32) * kr_new, axis=1, keepdims=True))
    state = (m0, jnp.ones_like(m0), jnp.broadcast_to(c_new, (N_HEADS, KV_LORA_RANK)))

    per = n_q // n_chunks
    pending = [prompt_scores(i) for i in range(SCORE_LOOKAHEAD)]
    sc_next = sample_scores(0)
    for qi in range(n_q):
        s_cur = pending.pop(0)
        if qi + SCORE_LOOKAHEAD < n_q:
            pending.append(prompt_scores(qi + SCORE_LOOKAHEAD))
        prompt_finish(qi, s_cur)
        if qi % per == per - 1:
            j = qi // per
            sc_cur = sc_next
            if j + 1 < n_chunks:
                sc_next = sample_scores(j + 1)
            state = sample_update(state, *sc_cur)
    _, denom, acc = state
    olat_ref[0] = (acc / denom).astype(BF16)

    @pl.when(s_idx == n_seq - 1)
    def _():
        wait_pages(1 - slot)


def _attention(page_table, q_all, kn, krkr, v, q_lat, q_rope, c_new, kr_new, cache_c, cache_krt,
               batch, seq):
    n_seq, n_pages = page_table.shape
    page_size = cache_c.shape[1]
    past = n_pages * page_size
    assert n_seq == batch * N_HEADS and past % KEY_CHUNK == 0 and seq % Q_TILE == 0
    assert (seq // Q_TILE) % (past // KEY_CHUNK) == 0

    def per_seq(width):
        return pl.BlockSpec((1,) + width, lambda b, h, pt: (b * N_HEADS + h, 0, 0))

    grid_spec = pltpu.PrefetchScalarGridSpec(
        num_scalar_prefetch=1,
        grid=(batch, N_HEADS),
        in_specs=[pl.BlockSpec((seq, HEAD_DIM_PAD), lambda b, h, pt: (b, h)),
                  pl.BlockSpec((seq, QK_NOPE_DIM), lambda b, h, pt: (b, h)),
                  pl.BlockSpec((seq, LANES), lambda b, h, pt: (b, 0)),
                  pl.BlockSpec((seq, V_HEAD_DIM), lambda b, h, pt: (b, h)),
                  per_seq((N_HEADS, KV_LORA_RANK)), per_seq((N_HEADS, QK_ROPE_DIM)),
                  per_seq((1, KV_LORA_RANK)), per_seq((1, QK_ROPE_DIM)),
                  pl.BlockSpec(memory_space=pl.ANY), pl.BlockSpec(memory_space=pl.ANY)],
        out_specs=[pl.BlockSpec((seq, V_HEAD_DIM), lambda b, h, pt: (b, h)),
                   per_seq((N_HEADS, KV_LORA_RANK))],
        scratch_shapes=[pltpu.VMEM((2, past, KV_LORA_RANK), F32),
                        pltpu.VMEM((2, QK_ROPE_DIM, past), F32),
                        pltpu.SemaphoreType.DMA((2, 2))],
    )
    return pl.pallas_call(
        functools.partial(_attn_kernel, n_pages=n_pages, page_size=page_size),
        grid_spec=grid_spec,
        out_shape=[jax.ShapeDtypeStruct((batch * seq, N_HEADS * V_HEAD_DIM), BF16),
                   jax.ShapeDtypeStruct((n_seq, N_HEADS, KV_LORA_RANK), BF16)],
        compiler_params=_params(2),
        name="attention",
    )(page_table, q_all, kn, krkr, v, q_lat, q_rope, c_new, kr_new, cache_c, cache_krt)


def _rope_table(pos):
    inv = ROPE_THETA ** (-jnp.arange(0, QK_ROPE_DIM, 2, dtype=F32) / QK_ROPE_DIM)
    ang = pos[:, None] * inv[None, :]
    cos, sin = jnp.cos(ang), jnp.sin(ang)
    reps = LANES // QK_ROPE_DIM
    return jnp.concatenate([cos, cos] * reps + [-sin, sin] * reps, axis=1)


def kernel(x_prompt, x_sample, cache_kv_latent, cache_k_rope, page_table, norm_mix, sgu_w_in,
           sgu_v_norm, sgu_w_s, sgu_b_s, sgu_w_out, norm_ffn, ffn_w_in, ffn_w_out, kv_norm, w_dkv,
           kv_latent_norm, w_uk, w_uv, w_dq, q_norm, w_uq, w_o, final_norm):
    batch, seq, _ = x_prompt.shape
    n_sample, dec_seq, _ = x_sample.shape
    n_pages = page_table.shape[1]
    page_size = cache_kv_latent.shape[1]
    past_len = n_pages * page_size
    assert dec_seq == 1 and past_len % CHUNK == 0 and seq % TOKEN_TILE == 0
    assert n_sample == CHUNK
    n_a = sgu_w_in.shape[0]
    n_b = w_dq.shape[0]
    t_prompt = batch * seq

    bf = lambda a: a.astype(BF16)
    row = lambda a: a.reshape(1, -1)
    w_dkv_ext = bf(jnp.concatenate([w_dkv, w_dkv[:, KV_LORA_RANK:]], axis=1))
    w_uk2 = bf(w_uk.reshape(KV_LORA_RANK, N_HEADS * QK_NOPE_DIM))
    w_uv2 = bf(w_uv.reshape(KV_LORA_RANK, N_HEADS * V_HEAD_DIM))
    w_ukt = bf(jnp.transpose(w_uk, (1, 2, 0)))
    w_uvt = bf(jnp.transpose(w_uv, (1, 0, 2)))
    w_uq3 = w_uq.reshape(n_b, Q_LORA_RANK, N_HEADS, QK_NOPE_DIM + QK_ROPE_DIM)
    w_uq_ext = bf(jnp.concatenate(
        [w_uq3[..., :QK_NOPE_DIM].reshape(n_b, Q_LORA_RANK, -1),
         w_uq3[..., QK_NOPE_DIM:].reshape(n_b, Q_LORA_RANK, -1)], axis=2))
    tab_p = _rope_table(jnp.arange(seq, dtype=F32))
    tab_s = jnp.broadcast_to(_rope_table(jnp.full((1,), past_len, F32)), (n_sample, 2 * LANES))
    cache_krt = jnp.swapaxes(cache_k_rope, 1, 2)

    rows = lambda a: a.reshape(a.shape[0], 1, a.shape[1])
    norm_mix, norm_ffn, sgu_v_norm, q_norm = map(rows, (norm_mix, norm_ffn, sgu_v_norm, q_norm))
    sgu_w_in, sgu_w_out, ffn_w_in, ffn_w_out, w_dq, w_o = map(
        bf, (sgu_w_in, sgu_w_out, ffn_w_in, ffn_w_out, w_dq, w_o))
    sgu_b_t = jnp.swapaxes(sgu_b_s, 1, 2)

    hp = x_prompt.reshape(t_prompt, D_MODEL)
    hs = x_sample.reshape(n_sample, D_MODEL)
    v_rows = []
    for l in range(n_a):
        hp, hs, v_s = _sgu_layer(hp, hs, (norm_mix, l), (sgu_w_in, l), (sgu_v_norm, l),
                                 (sgu_w_s, l), (sgu_b_t, l), (sgu_w_out, l))
        v_rows.append(v_s)
        hp, hs = _ffn_layer(hp, hs, (norm_ffn, l), (ffn_w_in, l), (ffn_w_out, l))

    c_p, kr_p, kn, krkr, v, c_s, kr_s = _kv_layer(hp, hs, tab_p, tab_s, row(kv_norm), w_dkv_ext,
                                                  row(kv_latent_norm), w_uk2, w_uv2)
    c_new = c_s.reshape(n_sample, 1, KV_LORA_RANK)
    kr_new = kr_s.reshape(n_sample, 1, QK_ROPE_DIM)

    for j in range(n_b):
        l = n_a + j
        q_all, q_lat, q_rope = _q_layer(hp, hs, tab_p, tab_s, (norm_mix, l), (w_dq, j),
                                        (q_norm, j), (w_uq_ext, j), w_ukt)
        o_p, o_lat = _attention(page_table, q_all, kn, krkr, v,
                                q_lat.reshape(n_sample, N_HEADS, KV_LORA_RANK),
                                q_rope.reshape(n_sample, N_HEADS, QK_ROPE_DIM), c_new, kr_new,
                                cache_kv_latent, cache_krt, batch, seq)
        hp, hs = _ffn_layer(hp, hs, (norm_ffn, l), (ffn_w_in, l), (ffn_w_out, l),
                            attn=(o_p, o_lat.reshape(n_sample, N_HEADS * KV_LORA_RANK), w_uvt,
                                  (w_o, j)),
                            final=row(final_norm) if j == n_b - 1 else None)

    return (hp.reshape(batch, seq, D_MODEL),
            hs.reshape(n_sample, 1, D_MODEL),
            c_p.reshape(batch, seq, KV_LORA_RANK),
            kr_p.reshape(batch, seq, QK_ROPE_DIM),
            c_new,
            kr_new,
            jnp.stack(v_rows).reshape(n_a, n_sample, 1, D_SGU))
```
